```python
import jax, jax.numpy as jnp
from jax import lax
import numpy as np

D_MODEL = 1024
BATCH = 4
SEQ = 4096
DEPTH = 4
DEC_BATCH = 128
DEC_SEQ = 4
PAST_LEN = 2048
PAGE_SIZE = 128

N_A = DEPTH // 2
N_B = DEPTH - N_A
D_FF = 2816
PLE_DIM = 256
CHUNK = 128
D_GMLP = D_MODEL
GMLP_GROUPS = 8
GMLP_GROUP_DIM = D_GMLP // GMLP_GROUPS
N_HEADS = 8
HEAD_DIM = D_MODEL // N_HEADS
D_ATT = N_HEADS * HEAD_DIM
Q_BLOCK = 128
EPS = 1e-6

kernel_name = 'yoco_gmlp_fox_macaron_decoder_step'


def rmsnorm(x, g):
    xf = x.astype(jnp.float32)
    y = xf * lax.rsqrt(jnp.mean(xf * xf, axis=-1, keepdims=True) + EPS)
    return (y * g.astype(jnp.float32)).astype(x.dtype)


def swiglu(x, w1, w3, w2):
    return (jax.nn.silu(x @ w1) * (x @ w3)) @ w2


def gmlp_mix(xn, w_in, v_norm, w_s, b_s, w_out):
    bsz, L, _ = xn.shape
    u, v = jnp.split(xn @ w_in, 2, axis=-1)
    v = rmsnorm(v, v_norm)
    c = min(L, CHUNK)
    n_ch = -(-L // c)
    pad = n_ch * c - L
    vp = jnp.pad(v, ((0, 0), (0, pad), (0, 0)))
    vg = vp.reshape(bsz, n_ch, c, GMLP_GROUPS, GMLP_GROUP_DIM)
    mask = jnp.tril(jnp.ones((c, c), dtype=bool)).astype(w_s.dtype)
    ws = w_s[:, :c, :c] * mask[None]
    s = jnp.einsum('gts,bnsgd->bntgd', ws, vg)
    s = s + b_s[:, :c].T[None, None, :, :, None]
    s = s.reshape(bsz, n_ch * c, D_GMLP)[:, :L]
    return (u * s) @ w_out, v


def shared_kv(h, kv_norm, w_kvf, b_f, k_norm):
    bsz, L, _ = h.shape
    xs = rmsnorm(h, kv_norm)
    kvf = xs @ w_kvf
    k = rmsnorm(kvf[..., :D_ATT].reshape(bsz, L, N_HEADS, HEAD_DIM), k_norm)
    v = kvf[..., D_ATT:2 * D_ATT].reshape(bsz, L, N_HEADS, HEAD_DIM)
    logf = jax.nn.log_sigmoid((kvf[..., 2 * D_ATT:] + b_f).astype(jnp.float32))
    return k, v, logf


def fox_prompt(q, k, v, logf):
    bsz, S, _, _ = q.shape
    scale = HEAD_DIM ** -0.5
    csum = jnp.cumsum(logf.astype(jnp.float32), axis=1).transpose(0, 2, 1)
    kpos = jnp.arange(S)

    def block(i):
        qs = i * Q_BLOCK
        qb = lax.dynamic_slice_in_dim(q, qs, Q_BLOCK, axis=1)
        cq = lax.dynamic_slice_in_dim(csum, qs, Q_BLOCK, axis=2)
        s = jnp.einsum('bqhd,bkhd->bhqk', qb, k).astype(jnp.float32) * scale
        s = s + cq[..., :, None] - csum[..., None, :]
        qpos = qs + jnp.arange(Q_BLOCK)
        causal = kpos[None, :] <= qpos[:, None]
        s = jnp.where(causal[None, None], s, -jnp.inf)
        p = jax.nn.softmax(s, axis=-1).astype(v.dtype)
        return jnp.einsum('bhqk,bkhd->bqhd', p, v)

    out = lax.map(block, jnp.arange(S // Q_BLOCK))
    return out.transpose(1, 0, 2, 3, 4).reshape(bsz, S, N_HEADS, HEAD_DIM)


def fox_sample(q, k_new, v_new, lf_new, k_past, v_past, lf_past):
    T = q.shape[1]
    P = k_past.shape[1]
    scale = HEAD_DIM ** -0.5
    c_new = jnp.cumsum(lf_new.astype(jnp.float32), axis=1).transpose(0, 2, 1)
    c_past = jnp.cumsum(lf_past.astype(jnp.float32), axis=1)
    tail = (c_past[:, -1:] - c_past).transpose(0, 2, 1)
    s_past = jnp.einsum('bqhd,bkhd->bhqk', q, k_past).astype(jnp.float32) * scale
    s_past = s_past + c_new[..., :, None] + tail[..., None, :]
    s_new = jnp.einsum('bqhd,bkhd->bhqk', q, k_new).astype(jnp.float32) * scale
    s_new = s_new + c_new[..., :, None] - c_new[..., None, :]
    causal = jnp.tril(jnp.ones((T, T), dtype=bool))
    s_new = jnp.where(causal[None, None], s_new, -jnp.inf)
    p = jax.nn.softmax(jnp.concatenate([s_past, s_new], axis=-1), axis=-1).astype(v_new.dtype)
    return (jnp.einsum('bhqk,bkhd->bqhd', p[..., :P], v_past)
            + jnp.einsum('bhqk,bkhd->bqhd', p[..., P:], v_new))


def setup_inputs(seed: int = 0) -> dict:
    key = jax.random.key(seed)
    ks = iter(jax.random.split(key, 48))
    f32 = jnp.float32

    def nrm(shape, scale):
        return jax.random.normal(next(ks), shape, f32) * scale

    def gain(shape):
        return 1.0 + 0.05 * jax.random.normal(next(ks), shape, f32)

    n_pages = PAST_LEN // PAGE_SIZE
    n_used = DEC_BATCH * n_pages
    n_pool = n_used + n_used // 4
    page_table = jax.random.permutation(next(ks), n_pool)[:n_used].reshape(DEC_BATCH, n_pages).astype(jnp.int32)
    d = D_MODEL
    return {
        'x_prompt': nrm((BATCH, SEQ, d), 1.0),
        'x_sample': nrm((DEC_BATCH, DEC_SEQ, d), 1.0),
        'p_prompt': nrm((DEPTH, BATCH, SEQ, PLE_DIM), 1.0),
        'p_sample': nrm((DEPTH, DEC_BATCH, DEC_SEQ, PLE_DIM), 1.0),
        'cache_k': nrm((n_pool, PAGE_SIZE, N_HEADS, HEAD_DIM), 1.0),
        'cache_v': nrm((n_pool, PAGE_SIZE, N_HEADS, HEAD_DIM), 1.0),
        'cache_logf': jax.nn.log_sigmoid(3.0 + nrm((n_pool, PAGE_SIZE, N_HEADS), 1.0)),
        'page_table': page_table,
        'ffn1_norm': gain((DEPTH, d)),
        'ffn1_w1': nrm((DEPTH, d, D_FF), d ** -0.5),
        'ffn1_w3': nrm((DEPTH, d, D_FF), d ** -0.5),
        'ffn1_w2': nrm((DEPTH, D_FF, d), D_FF ** -0.5),
        'mix_norm': gain((DEPTH, d)),
        'ffn2_norm': gain((DEPTH, d)),
        'ffn2_w1': nrm((DEPTH, d, D_FF), d ** -0.5),
        'ffn2_w3': nrm((DEPTH, d, D_FF), d ** -0.5),
        'ffn2_w2': nrm((DEPTH, D_FF, d), D_FF ** -0.5),
        'ple_norm': gain((DEPTH, d)),
        'ple_w_gate': nrm((DEPTH, d, d), d ** -0.5),
        'ple_w_proj': nrm((DEPTH, PLE_DIM, d), PLE_DIM ** -0.5),
        'gmlp_w_in': nrm((N_A, d, 2 * D_GMLP), d ** -0.5),
        'gmlp_v_norm': gain((N_A, D_GMLP)),
        'gmlp_w_s': nrm((N_A, GMLP_GROUPS, CHUNK, CHUNK), CHUNK ** -0.5),
        'gmlp_b_s': 1.0 + 0.1 * jax.random.normal(next(ks), (N_A, GMLP_GROUPS, CHUNK), f32),
        'gmlp_w_out': nrm((N_A, D_GMLP, d), D_GMLP ** -0.5),
        'kv_norm': gain((d,)),
        'w_kvf': nrm((d, 2 * D_ATT + N_HEADS), d ** -0.5),
        'b_f': 3.0 + 0.5 * jax.random.normal(next(ks), (N_HEADS,), f32),
        'k_norm': gain((HEAD_DIM,)),
        'att_w_q': nrm((N_B, d, D_ATT), d ** -0.5),
        'q_norm': gain((N_B, HEAD_DIM)),
        'att_w_o': nrm((N_B, D_ATT, d), D_ATT ** -0.5),
    }


def reference(x_prompt, x_sample, p_prompt, p_sample, cache_k, cache_v, cache_logf, page_table,
              ffn1_norm, ffn1_w1, ffn1_w3, ffn1_w2, mix_norm, ffn2_norm, ffn2_w1, ffn2_w3, ffn2_w2,
              ple_norm, ple_w_gate, ple_w_proj, gmlp_w_in, gmlp_v_norm, gmlp_w_s, gmlp_b_s, gmlp_w_out,
              kv_norm, w_kvf, b_f, k_norm, att_w_q, q_norm, att_w_o):

    def trunk(x, pe, past):
        h = x
        bsz, L, _ = x.shape
        kv = None
        gmlp_vs = []
        for i in range(DEPTH):
            h = h + 0.5 * swiglu(rmsnorm(h, ffn1_norm[i]), ffn1_w1[i], ffn1_w3[i], ffn1_w2[i])
            xn = rmsnorm(h, mix_norm[i])
            if i < N_A:
                mix, v_rows = gmlp_mix(xn, gmlp_w_in[i], gmlp_v_norm[i], gmlp_w_s[i], gmlp_b_s[i], gmlp_w_out[i])
                gmlp_vs.append(v_rows)
            else:
                j = i - N_A
                q = rmsnorm((xn @ att_w_q[j]).reshape(bsz, L, N_HEADS, HEAD_DIM), q_norm[j])
                k, v, lf = kv
                if past is None:
                    o = fox_prompt(q, k, v, lf)
                else:
                    o = fox_sample(q, k, v, lf, past[0], past[1], past[2])
                mix = o.reshape(bsz, L, D_ATT) @ att_w_o[j]
            h = h + mix
            h = h + 0.5 * swiglu(rmsnorm(h, ffn2_norm[i]), ffn2_w1[i], ffn2_w3[i], ffn2_w2[i])
            gate = jax.nn.sigmoid(rmsnorm(h, ple_norm[i]) @ ple_w_gate[i])
            h = h + gate * (pe[i] @ ple_w_proj[i])
            if i == N_A - 1:
                kv = shared_kv(h, kv_norm, w_kvf, b_f, k_norm)
        return h, kv, jnp.stack(gmlp_vs)

    y_prompt, (k_prompt, v_prompt, logf_prompt), _ = trunk(x_prompt, p_prompt, None)

    bd, n_pages = page_table.shape
    past_len = n_pages * PAGE_SIZE
    k_past = cache_k[page_table].reshape(bd, past_len, N_HEADS, HEAD_DIM)
    v_past = cache_v[page_table].reshape(bd, past_len, N_HEADS, HEAD_DIM)
    lf_past = cache_logf[page_table].reshape(bd, past_len, N_HEADS)
    y_sample, (k_sample, v_sample, logf_sample), gmlp_v_sample = trunk(x_sample, p_sample, (k_past, v_past, lf_past))

    return (y_prompt, y_sample, k_prompt, v_prompt, logf_prompt, k_sample, v_sample, logf_sample, gmlp_v_sample)
```

```python
import functools

import jax
import jax.numpy as jnp
from jax import lax
from jax.experimental import pallas as pl
from jax.experimental.pallas import tpu as pltpu

D = 1024
DFF = 2816
NH = 8
DH = 128
PLE = 256
DEPTH = 4
N_A = 2
BATCH, SEQ = 4, 4096
DEC_B, DEC_T = 128, 4
PAGE = 128
N_PAGES = 16
NP = BATCH * SEQ
NS = DEC_B * DEC_T
N = NP + NS
EPS = 1e-6
SCALE = DH ** -0.5

BF = jnp.bfloat16
F32 = jnp.float32

TM = 512
N_TILES = N // TM
FFN_TM = 768
FFN_TF = 256
TQ = 512
PAGES_PER_STEP = 4
VMEM_LIMIT = 56 * 1024 * 1024

NT_DIMS = (((1,), (1,)), ((), ()))


def _params(*sem):
    return pltpu.CompilerParams(dimension_semantics=sem, vmem_limit_bytes=VMEM_LIMIT)


def _rms(x, g):
    ms = jnp.mean(x * x, axis=-1, keepdims=True)
    return x * lax.rsqrt(ms + EPS) * g


def _head_rms(x, g):
    outs = []
    for h in range(NH):
        xh = x[:, h * DH:(h + 1) * DH]
        ms = jnp.mean(xh * xh, axis=-1, keepdims=True)
        outs.append(xh * lax.rsqrt(ms + EPS))
    return jnp.concatenate(outs, axis=-1) * g


def _dot(a, b):
    return jnp.dot(a, b, preferred_element_type=F32)


def _dot_nt(a, b):
    return lax.dot_general(a, b, NT_DIMS, preferred_element_type=F32)


def _ffn_kernel(x_ref, g_ref, w1_ref, w3_ref, w2_ref, o_ref, xn_ref, acc_ref):
    j = pl.program_id(1)

    @pl.when(j == 0)
    def _():
        xn_ref[...] = _rms(x_ref[...], g_ref[...]).astype(BF)

    xn = xn_ref[...]
    h1 = _dot(xn, w1_ref[...])
    h3 = _dot(xn, w3_ref[...])
    a = (h1 * jax.nn.sigmoid(h1) * h3).astype(BF)
    part = _dot(a, w2_ref[...])

    @pl.when(j == 0)
    def _():
        acc_ref[...] = part

    @pl.when(j > 0)
    def _():
        acc_ref[...] += part

    @pl.when(j == pl.num_programs(1) - 1)
    def _():
        o_ref[...] = x_ref[...] + 0.5 * acc_ref[...]


def _ffn(h, g, w1, w3, w2):
    tm, tf = FFN_TM, FFN_TF
    return pl.pallas_call(
        _ffn_kernel,
        grid=(N // tm, DFF // tf),
        in_specs=[
            pl.BlockSpec((tm, D), lambda i, j: (i, 0)),
            pl.BlockSpec((1, D), lambda i, j: (0, 0)),
            pl.BlockSpec((D, tf), lambda i, j: (0, j)),
            pl.BlockSpec((D, tf), lambda i, j: (0, j)),
            pl.BlockSpec((tf, D), lambda i, j: (j, 0)),
        ],
        out_specs=pl.BlockSpec((tm, D), lambda i, j: (i, 0)),
        out_shape=jax.ShapeDtypeStruct((N, D), F32),
        scratch_shapes=[pltpu.VMEM((tm, D), BF), pltpu.VMEM((tm, D), F32)],
        compiler_params=_params("parallel", "arbitrary"),
        name="ffn",
    )(h, g, w1, w3, w2)


def _gmlp_kernel(x_ref, g_ref, win_ref, vn_ref, wmix_ref, bmix_ref, wout_ref, o_ref, vs_ref):
    i = pl.program_id(0)
    x = x_ref[...]
    xn = _rms(x, g_ref[...]).astype(BF)
    uv = _dot(xn, win_ref[...])
    u = uv[:, :D]
    v = _rms(uv[:, D:], vn_ref[...])

    @pl.when(i == N_TILES - 1)
    def _():
        vs_ref[...] = v

    vb = v.astype(BF)
    bias = bmix_ref[...]
    rows = []
    for c in range(TM // DH):
        cols = []
        for g in range(NH):
            blk = vb[c * DH:(c + 1) * DH, g * DH:(g + 1) * DH]
            cols.append(_dot(wmix_ref[g], blk))
        rows.append(jnp.concatenate(cols, axis=1) + bias)
    s = jnp.concatenate(rows, axis=0)
    m = (u * s).astype(BF)
    o_ref[...] = x + _dot(m, wout_ref[...])


def _gmlp(h, g, w_in, v_norm, wmix, bmix, w_out):
    const = lambda i: (0, 0)
    kind = lambda i: (i // (N_TILES - 1), 0, 0, 0)
    return pl.pallas_call(
        _gmlp_kernel,
        grid=(N_TILES,),
        in_specs=[
            pl.BlockSpec((TM, D), lambda i: (i, 0)),
            pl.BlockSpec((1, D), const),
            pl.BlockSpec((D, 2 * D), const),
            pl.BlockSpec((1, D), const),
            pl.BlockSpec((None, NH, DH, DH), kind),
            pl.BlockSpec((None, DH, D), lambda i: (i // (N_TILES - 1), 0, 0)),
            pl.BlockSpec((D, D), const),
        ],
        out_specs=[
            pl.BlockSpec((TM, D), lambda i: (i, 0)),
            pl.BlockSpec((NS, D), const),
        ],
        out_shape=[
            jax.ShapeDtypeStruct((N, D), F32),
            jax.ShapeDtypeStruct((NS, D), F32),
        ],
        compiler_params=_params("arbitrary"),
        name="gmlp",
    )(h, g, w_in, v_norm, wmix, bmix, w_out)


def _ple_kernel(x_ref, g_ref, wg_ref, pe_ref, wp_ref, o_ref):
    x = x_ref[...]
    xn = _rms(x, g_ref[...]).astype(BF)
    gate = jax.nn.sigmoid(_dot(xn, wg_ref[...]))
    proj = _dot(pe_ref[...].astype(BF), wp_ref[...])
    o_ref[...] = x + gate * proj


def _ple(h, g, w_gate, pe, layer, w_proj):
    const = lambda i: (0, 0)
    return pl.pallas_call(
        _ple_kernel,
        grid=(N_TILES,),
        in_specs=[
            pl.BlockSpec((TM, D), lambda i: (i, 0)),
            pl.BlockSpec((1, D), const),
            pl.BlockSpec((D, D), const),
            pl.BlockSpec((None, TM, PLE), lambda i: (layer, i, 0)),
            pl.BlockSpec((PLE, D), const),
        ],
        out_specs=pl.BlockSpec((TM, D), lambda i: (i, 0)),
        out_shape=jax.ShapeDtypeStruct((N, D), F32),
        compiler_params=_params("parallel"),
        name="ple",
    )(h, g, w_gate, pe, w_proj)


def _kv_kernel(x_ref, g_ref, wk_ref, wv_ref, wf_ref, bf_ref, kn_ref,
               k32_ref, v32_ref, kb_ref, vb_ref, lf_ref, cs_ref, carry_ref):
    i = pl.program_id(0)
    is_sample = i == N_TILES - 1
    xs = _rms(x_ref[...], g_ref[...]).astype(BF)
    k = _head_rms(_dot(xs, wk_ref[...]), kn_ref[...])
    v = _dot(xs, wv_ref[...])
    k32_ref[...] = k
    v32_ref[...] = v
    kb_ref[...] = k.astype(BF)
    vb_ref[...] = v.astype(BF)

    f = _dot_nt(wf_ref[...], xs)[:NH] + bf_ref[...]
    lf = jnp.minimum(f, 0.0) - jnp.log1p(jnp.exp(-jnp.abs(f)))
    lf_ref[...] = lf

    lane = lax.broadcasted_iota(jnp.int32, (NH, TM), 1)
    pos = lane & jnp.where(is_sample, DEC_T - 1, TM - 1)
    y = lf
    sh = 1
    while sh < TM:
        y = y + jnp.where(pos >= sh, pltpu.roll(y, sh, 1), 0.0)
        sh *= 2

    @pl.when(jnp.logical_or(i % (SEQ // TM) == 0, is_sample))
    def _():
        carry_ref[...] = jnp.zeros_like(carry_ref)

    y = y + carry_ref[:, 0:1]
    cs_ref[...] = y
    carry_ref[...] = jnp.broadcast_to(y[:, TM - 1:TM], carry_ref.shape)


def _kv(h, g, wk, wv, wf, bf, kn):
    const = lambda i: (0, 0)
    tok = lambda i: (i, 0)
    lanes = lambda i: (0, i)
    return pl.pallas_call(
        _kv_kernel,
        grid=(N_TILES,),
        in_specs=[
            pl.BlockSpec((TM, D), tok),
            pl.BlockSpec((1, D), const),
            pl.BlockSpec((D, D), const),
            pl.BlockSpec((D, D), const),
            pl.BlockSpec((16, D), const),
            pl.BlockSpec((NH, 1), const),
            pl.BlockSpec((1, D), const),
        ],
        out_specs=[
            pl.BlockSpec((TM, D), tok),
            pl.BlockSpec((TM, D), tok),
            pl.BlockSpec((TM, D), tok),
            pl.BlockSpec((TM, D), tok),
            pl.BlockSpec((NH, TM), lanes),
            pl.BlockSpec((NH, TM), lanes),
        ],
        out_shape=[
            jax.ShapeDtypeStruct((N, D), F32),
            jax.ShapeDtypeStruct((N, D), F32),
            jax.ShapeDtypeStruct((N, D), BF),
            jax.ShapeDtypeStruct((N, D), BF),
            jax.ShapeDtypeStruct((NH, N), F32),
            jax.ShapeDtypeStruct((NH, N), F32),
        ],
        scratch_shapes=[pltpu.VMEM((NH, 128), F32)],
        compiler_params=_params("arbitrary"),
        name="shared_kv",
    )(h, g, wk, wv, wf, bf, kn)


def _qproj_kernel(x_ref, g_ref, wq_ref, qn_ref, q_ref):
    xn = _rms(x_ref[...], g_ref[...]).astype(BF)
    q_ref[...] = _head_rms(_dot(xn, wq_ref[...]), qn_ref[...]).astype(BF)


def _qproj(h, g, wq, qn):
    const = lambda i: (0, 0)
    return pl.pallas_call(
        _qproj_kernel,
        grid=(N_TILES,),
        in_specs=[
            pl.BlockSpec((TM, D), lambda i: (i, 0)),
            pl.BlockSpec((1, D), const),
            pl.BlockSpec((D, D), const),
            pl.BlockSpec((1, D), const),
        ],
        out_specs=pl.BlockSpec((TM, D), lambda i: (i, 0)),
        out_shape=jax.ShapeDtypeStruct((N, D), BF),
        compiler_params=_params("parallel"),
        name="qproj",
    )(h, g, wq, qn)


def _oproj_kernel(x_ref, o_ref, wo_ref, y_ref):
    y_ref[...] = x_ref[...] + _dot(o_ref[...], wo_ref[...])


def _oproj(h, o, wo):
    return pl.pallas_call(
        _oproj_kernel,
        grid=(N_TILES,),
        in_specs=[
            pl.BlockSpec((TM, D), lambda i: (i, 0)),
            pl.BlockSpec((TM, D), lambda i: (i, 0)),
            pl.BlockSpec((D, D), lambda i: (0, 0)),
        ],
        out_specs=pl.BlockSpec((TM, D), lambda i: (i, 0)),
        out_shape=jax.ShapeDtypeStruct((N, D), F32),
        compiler_params=_params("parallel"),
        name="oproj",
    )(h, o, wo)


def _attn_prompt_kernel(q_ref, k_ref, v_ref, c_ref, o_ref, m_ref, l_ref, acc_ref):
    qi = pl.program_id(2)
    q = q_ref[...]
    q0 = pl.multiple_of(qi * TQ, TQ)
    cq_row = c_ref[0, :, pl.ds(q0, TQ)]
    cq = jnp.transpose(jnp.broadcast_to(cq_row, (8, TQ)))[:, 0:1]

    m_ref[...] = jnp.full_like(m_ref, -jnp.inf)
    l_ref[...] = jnp.zeros_like(l_ref)
    acc_ref[...] = jnp.zeros_like(acc_ref)

    def step(kj, diagonal):
        k0 = pl.multiple_of(kj * TQ, TQ)
        k = k_ref[pl.ds(k0, TQ), :]
        v = v_ref[pl.ds(k0, TQ), :]
        s = _dot_nt(q, k) * SCALE
        s = s + cq - c_ref[0, :, pl.ds(k0, TQ)]
        if diagonal:
            row = lax.broadcasted_iota(jnp.int32, (TQ, TQ), 0)
            col = lax.broadcasted_iota(jnp.int32, (TQ, TQ), 1)
            s = jnp.where(col <= row, s, -jnp.inf)
        m_prev = m_ref[...]
        m_new = jnp.maximum(m_prev, jnp.max(s, axis=1, keepdims=True))
        alpha = jnp.exp(m_prev - m_new)
        p = jnp.exp(s - m_new)
        l_ref[...] = alpha * l_ref[...] + jnp.sum(p, axis=1, keepdims=True)
        acc_ref[...] = alpha * acc_ref[...] + _dot(p.astype(BF), v)
        m_ref[...] = m_new

    def body(kj, carry):
        step(kj, False)
        return carry

    lax.fori_loop(0, qi, body, 0)
    step(qi, True)
    o_ref[...] = (acc_ref[...] / l_ref[...]).astype(BF)


def _attn_prompt(q, kb, vb, cs3):
    nq = SEQ // TQ
    return pl.pallas_call(
        _attn_prompt_kernel,
        grid=(BATCH, NH, nq),
        in_specs=[
            pl.BlockSpec((TQ, DH), lambda b, h, qi: (b * nq + qi, h)),
            pl.BlockSpec((SEQ, DH), lambda b, h, qi: (b, h)),
            pl.BlockSpec((SEQ, DH), lambda b, h, qi: (b, h)),
            pl.BlockSpec((1, 1, SEQ), lambda b, h, qi: (h, 0, b)),
        ],
        out_specs=pl.BlockSpec((TQ, DH), lambda b, h, qi: (b * nq + qi, h)),
        out_shape=jax.ShapeDtypeStruct((NP, D), BF),
        scratch_shapes=[
            pltpu.VMEM((TQ, 1), F32),
            pltpu.VMEM((TQ, 1), F32),
            pltpu.VMEM((TQ, DH), F32),
        ],
        compiler_params=_params("parallel", "parallel", "arbitrary"),
        name="attn_prompt",
    )(q, kb, vb, cs3)


def _suffix_kernel(x_ref, excl_ref, tot_ref):
    x = x_ref[...]
    lane = lax.broadcasted_iota(jnp.int32, x.shape, 1)
    y = x
    sh = 1
    while sh < PAGE:
        y = y + jnp.where(lane < PAGE - sh, pltpu.roll(y, PAGE - sh, 1), 0.0)
        sh *= 2
    excl_ref[...] = y - x
    tot_ref[...] = jnp.broadcast_to(y[:, 0:1], x.shape)


def _suffix(lf_rows):
    rows = lf_rows.shape[0]
    tr = 2048
    spec = pl.BlockSpec((tr, PAGE), lambda i: (i, 0))
    return pl.pallas_call(
        _suffix_kernel,
        grid=(rows // tr,),
        in_specs=[spec],
        out_specs=[spec, spec],
        out_shape=[jax.ShapeDtypeStruct((rows, PAGE), F32)] * 2,
        compiler_params=_params("parallel"),
        name="cache_suffix",
    )(lf_rows)


def _attn_sample_kernel(pt_ref, qbd_ref, kn_ref, vn_ref, cq_ref, ckn_ref, *rest):
    pp = PAGES_PER_STEP
    k_refs, v_refs = rest[:pp], rest[pp:2 * pp]
    e_refs, t_refs = rest[2 * pp:3 * pp], rest[3 * pp:4 * pp]
    o_ref, m_ref, l_ref, acc_ref, carry_ref = rest[4 * pp:]
    p = pl.program_id(1)
    rows = DEC_T * NH
    qbd = qbd_ref[0]
    cq = cq_ref[0][:, 0:1]

    @pl.when(p == 0)
    def _():
        s = _dot_nt(qbd, kn_ref[0]) * SCALE
        s = s + cq - ckn_ref[0][:, :16]
        row = lax.broadcasted_iota(jnp.int32, s.shape, 0)
        col = lax.broadcasted_iota(jnp.int32, s.shape, 1)
        s = jnp.where(col <= (row >> 3), s, -jnp.inf)
        m = jnp.max(s, axis=1, keepdims=True)
        e = jnp.exp(s - m)
        m_ref[...] = m
        l_ref[...] = jnp.sum(e, axis=1, keepdims=True)
        acc_ref[...] = _dot(e.astype(BF), vn_ref[0])
        carry_ref[...] = jnp.zeros_like(carry_ref)

    carry = carry_ref[...]
    biases = []
    for j in range(pp):
        biases.append(e_refs[j][0] + carry)
        carry = carry + t_refs[j][0]
    carry_ref[...] = carry
    bias = jnp.concatenate(biases, axis=1)
    bias = jnp.concatenate([bias] * DEC_T, axis=0)
    k = jnp.concatenate([r[0].astype(BF) for r in k_refs], axis=0)
    v = jnp.concatenate([r[0].astype(BF) for r in v_refs], axis=0)
    s = _dot_nt(qbd, k) * SCALE
    s = s + cq + bias
    m_prev = m_ref[...]
    m_new = jnp.maximum(m_prev, jnp.max(s, axis=1, keepdims=True))
    alpha = jnp.exp(m_prev - m_new)
    e = jnp.exp(s - m_new)
    l_ref[...] = alpha * l_ref[...] + jnp.sum(e, axis=1, keepdims=True)
    acc_ref[...] = alpha * acc_ref[...] + _dot(e.astype(BF), v)
    m_ref[...] = m_new

    @pl.when(p == pl.num_programs(1) - 1)
    def _():
        a = acc_ref[...] / l_ref[...]
        row = lax.broadcasted_iota(jnp.int32, (rows, D), 0)
        col = lax.broadcasted_iota(jnp.int32, (rows, D), 1)
        a = jnp.where((col >> 7) == (row & 7), a, 0.0)
        outs = [jnp.sum(a[t * NH:(t + 1) * NH], axis=0, keepdims=True) for t in range(DEC_T)]
        o_ref[0] = jnp.concatenate(outs, axis=0)


def _attn_sample(page_table, qbd, kn, vn, cq, ckn, cache_k, cache_v, excl, tot):
    pp = PAGES_PER_STEP
    rows = DEC_T * NH

    def seq(b, p, pt):
        return (b, 0, 0)

    def page(j):
        return lambda b, p, pt: (pt[b, N_PAGES - 1 - (pp * p + j)], 0, 0)

    in_specs = [
        pl.BlockSpec((1, rows, D), seq),
        pl.BlockSpec((1, 16, D), seq),
        pl.BlockSpec((1, 16, D), seq),
        pl.BlockSpec((1, rows, 128), seq),
        pl.BlockSpec((1, rows, 128), seq),
    ]
    in_specs += [pl.BlockSpec((1, PAGE, D), page(j)) for j in range(pp)]
    in_specs += [pl.BlockSpec((1, PAGE, D), page(j)) for j in range(pp)]
    in_specs += [pl.BlockSpec((1, NH, PAGE), page(j)) for j in range(pp)]
    in_specs += [pl.BlockSpec((1, NH, PAGE), page(j)) for j in range(pp)]
    grid_spec = pltpu.PrefetchScalarGridSpec(
        num_scalar_prefetch=1,
        grid=(DEC_B, N_PAGES // pp),
        in_specs=in_specs,
        out_specs=pl.BlockSpec((1, DEC_T, D), seq),
        scratch_shapes=[
            pltpu.VMEM((rows, 1), F32),
            pltpu.VMEM((rows, 1), F32),
            pltpu.VMEM((rows, D), F32),
            pltpu.VMEM((NH, PAGE), F32),
        ],
    )
    return pl.pallas_call(
        _attn_sample_kernel,
        grid_spec=grid_spec,
        out_shape=jax.ShapeDtypeStruct((DEC_B, DEC_T, D), F32),
        compiler_params=_params("parallel", "arbitrary"),
        name="attn_sample",
    )(page_table, qbd, kn, vn, cq, ckn,
      *([cache_k] * pp), *([cache_v] * pp), *([excl] * pp), *([tot] * pp))


def kernel(x_prompt, x_sample, p_prompt, p_sample, cache_k, cache_v, cache_logf, page_table,
           ffn1_norm, ffn1_w1, ffn1_w3, ffn1_w2, mix_norm, ffn2_norm, ffn2_w1, ffn2_w3, ffn2_w2,
           ple_norm, ple_w_gate, ple_w_proj, gmlp_w_in, gmlp_v_norm, gmlp_w_s, gmlp_b_s, gmlp_w_out,
           kv_norm, w_kvf, b_f, k_norm, att_w_q, q_norm, att_w_o):
    bf = lambda w: w.astype(BF)
    row = lambda g: g.reshape(1, -1)

    h = jnp.concatenate([x_prompt.reshape(NP, D), x_sample.reshape(NS, D)], axis=0)
    pe = jnp.concatenate([p_prompt.reshape(DEPTH, NP, PLE), p_sample.reshape(DEPTH, NS, PLE)], axis=1)

    tril = jnp.tril(jnp.ones((DH, DH), F32))
    w_prompt = gmlp_w_s * tril
    small = (gmlp_w_s[:, :, :DEC_T, :DEC_T] * tril[:DEC_T, :DEC_T])
    eye = jnp.eye(DH // DEC_T, dtype=F32)
    w_sample = jnp.einsum('ab,lgts->lgatbs', eye, small).reshape(N_A, NH, DH, DH)
    wmix = bf(jnp.stack([w_prompt, w_sample], axis=1))
    b_prompt = jnp.repeat(gmlp_b_s.transpose(0, 2, 1), DH, axis=2)
    b_sample = jnp.tile(b_prompt[:, :DEC_T], (1, DH // DEC_T, 1))
    bmix = jnp.stack([b_prompt, b_sample], axis=1)

    wk = bf(w_kvf[:, :D])
    wv = bf(w_kvf[:, D:2 * D])
    wf = bf(jnp.pad(w_kvf[:, 2 * D:].T, ((0, 16 - NH), (0, 0))))

    kv = None
    gmlp_vs = []
    for i in range(DEPTH):
        h = _ffn(h, row(ffn1_norm[i]), bf(ffn1_w1[i]), bf(ffn1_w3[i]), bf(ffn1_w2[i]))
        if i < N_A:
            h, v_rows = _gmlp(h, row(mix_norm[i]), bf(gmlp_w_in[i]), row(gmlp_v_norm[i]),
                              wmix[i], bmix[i], bf(gmlp_w_out[i]))
            gmlp_vs.append(v_rows.reshape(DEC_B, DEC_T, D))
        else:
            j = i - N_A
            k32, v32, kb, vb, lft, cst, excl, tot = kv
            q = _qproj(h, row(mix_norm[i]), bf(att_w_q[j]), row(jnp.tile(q_norm[j], NH)))
            o_p = _attn_prompt(q, kb, vb, cst.reshape(NH, 1, N))
            head_mask = (jnp.arange(D)[None, :] // DH == jnp.arange(NH)[:, None]).astype(BF)
            qbd = (q[NP:].reshape(DEC_B, DEC_T, 1, D) * head_mask[None, None]).reshape(DEC_B, DEC_T * NH, D)
            pad_new = lambda a: jnp.pad(a[NP:].reshape(DEC_B, DEC_T, D), ((0, 0), (0, 16 - DEC_T), (0, 0)))
            c_new = cst[:, NP:].reshape(NH, DEC_B, DEC_T).transpose(1, 2, 0)
            cq = jnp.broadcast_to(c_new.reshape(DEC_B, DEC_T * NH, 1), (DEC_B, DEC_T * NH, 128))
            ckn = jnp.broadcast_to(c_new.transpose(0, 2, 1)[:, None], (DEC_B, DEC_T, NH, DEC_T))
            ckn = jnp.pad(ckn.reshape(DEC_B, DEC_T * NH, DEC_T), ((0, 0), (0, 0), (0, 128 - DEC_T)))
            o_s = _attn_sample(page_table, qbd, pad_new(kb), pad_new(vb), cq, ckn,
                               cache_k.reshape(-1, PAGE, D), cache_v.reshape(-1, PAGE, D), excl, tot)
            o = jnp.concatenate([o_p, o_s.reshape(NS, D).astype(BF)], axis=0)
            h = _oproj(h, o, bf(att_w_o[j]))
        h = _ffn(h, row(ffn2_norm[i]), bf(ffn2_w1[i]), bf(ffn2_w3[i]), bf(ffn2_w2[i]))
        h = _ple(h, row(ple_norm[i]), bf(ple_w_gate[i]), pe, i, bf(ple_w_proj[i]))
        if i == N_A - 1:
            k32, v32, kb, vb, lft, cst = _kv(h, row(kv_norm), wk, wv, wf, b_f.reshape(NH, 1),
                                             row(jnp.tile(k_norm, NH)))
            n_pool = cache_logf.shape[0]
            lf_rows = cache_logf.transpose(0, 2, 1).reshape(n_pool * NH, PAGE)
            excl, tot = _suffix(lf_rows)
            kv = (k32, v32, kb, vb, lft, cst,
                  excl.reshape(n_pool, NH, PAGE), tot.reshape(n_pool, NH, PAGE))

    k32, v32, kb, vb, lft, cst, excl, tot = kv
    lf_tok = lft.T
    return (
        h[:NP].reshape(BATCH, SEQ, D),
        h[NP:].reshape(DEC_B, DEC_T, D),
        k32[:NP].reshape(BATCH, SEQ, NH, DH),
        v32[:NP].reshape(BATCH, SEQ, NH, DH),
        lf_tok[:NP].reshape(BATCH, SEQ, NH),
        k32[NP:].reshape(DEC_B, DEC_T, NH, DH),
        v32[NP:].reshape(DEC_B, DEC_T, NH, DH),
        lf_tok[NP:].reshape(DEC_B, DEC_T, NH),
        jnp.stack(gmlp_vs),
    )
```

```python
import jax
import jax.numpy as jnp
from jax import lax
from jax.experimental import pallas as pl
from jax.experimental.pallas import tpu as pltpu

D = 1024
DFF = 2816
NH = 8
DH = 128
PLE = 256
DEPTH = 4
N_A = 2
BATCH, SEQ = 4, 4096
DEC_B, DEC_T = 128, 4
PAGE = 128
N_PAGES = 16
NP = BATCH * SEQ
NS = DEC_B * DEC_T
N = NP + NS
EPS = 1e-6
LOG2E = 1.4426950408889634
QSCALE = DH ** -0.5 * LOG2E

BF = jnp.bfloat16
F32 = jnp.float32

TM = 512
N_TILES = N // TM
FFN_TM = 768
FFN_TF = 256
TQ = 512
HB = 2
PAGES_PER_STEP = 8
ROWS = DEC_T * NH
VMEM_LIMIT = 56 * 1024 * 1024

NT_DIMS = (((1,), (1,)), ((), ()))


def _params(*sem):
    return pltpu.CompilerParams(dimension_semantics=sem, vmem_limit_bytes=VMEM_LIMIT)


def _resident(shape):
    return pl.BlockSpec(shape, lambda *_: (0,) * len(shape), pipeline_mode=pl.Buffered(1))


def _rms(x, g):
    ms = jnp.mean(x * x, axis=-1, keepdims=True)
    return x * lax.rsqrt(ms + EPS) * g


def _head_rms(x, g):
    outs = []
    for h in range(NH):
        xh = x[:, h * DH:(h + 1) * DH]
        ms = jnp.mean(xh * xh, axis=-1, keepdims=True)
        outs.append(xh * lax.rsqrt(ms + EPS))
    return jnp.concatenate(outs, axis=-1) * g


def _dot(a, b):
    return jnp.dot(a, b, preferred_element_type=F32)


def _dot_nt(a, b):
    return lax.dot_general(a, b, NT_DIMS, preferred_element_type=F32)


def _split3(x):
    hi = x.astype(BF).astype(F32)
    r = x - hi
    mid = r.astype(BF).astype(F32)
    return hi, mid, r - mid


def _ffn_kernel(x_ref, g_ref, w1_ref, w3_ref, w2_ref, o_ref, xn_ref, acc_ref):
    xn_ref[...] = _rms(x_ref[...], g_ref[...]).astype(BF)
    acc_ref[...] = jnp.zeros_like(acc_ref)

    def body(j, carry):
        f0 = pl.multiple_of(j * FFN_TF, FFN_TF)
        xn = xn_ref[...]
        h1 = _dot(xn, w1_ref[:, pl.ds(f0, FFN_TF)])
        h3 = _dot(xn, w3_ref[:, pl.ds(f0, FFN_TF)])
        a = (h1 * jax.nn.sigmoid(h1) * h3).astype(BF)
        acc_ref[...] += _dot(a, w2_ref[pl.ds(f0, FFN_TF), :])
        return carry

    lax.fori_loop(0, DFF // FFN_TF, body, 0)
    o_ref[...] = x_ref[...] + 0.5 * acc_ref[...]


def _ffn(h, g, w1, w3, w2):
    tm = FFN_TM
    return pl.pallas_call(
        _ffn_kernel,
        grid=(N // tm,),
        in_specs=[
            pl.BlockSpec((tm, D), lambda i: (i, 0)),
            _resident((1, D)),
            _resident((D, DFF)),
            _resident((D, DFF)),
            _resident((DFF, D)),
        ],
        out_specs=pl.BlockSpec((tm, D), lambda i: (i, 0)),
        out_shape=jax.ShapeDtypeStruct((N, D), F32),
        scratch_shapes=[pltpu.VMEM((tm, D), BF), pltpu.VMEM((tm, D), F32)],
        compiler_params=_params("parallel"),
        name="ffn",
    )(h, g, w1, w3, w2)


def _gmlp_kernel(x_ref, g_ref, win_ref, vn_ref, wmix_ref, bmix_ref, wout_ref, o_ref, vs_ref):
    i = pl.program_id(0)
    x = x_ref[...]
    xn = _rms(x, g_ref[...]).astype(BF)
    uv = _dot(xn, win_ref[...])
    u = uv[:, :D]
    v = _rms(uv[:, D:], vn_ref[...])

    @pl.when(i == N_TILES - 1)
    def _():
        vs_ref[...] = v

    vb = v.astype(BF)
    bias = bmix_ref[...]
    rows = []
    for c in range(TM // DH):
        cols = []
        for g in range(NH):
            blk = vb[c * DH:(c + 1) * DH, g * DH:(g + 1) * DH]
            cols.append(_dot(wmix_ref[g], blk))
        rows.append(jnp.concatenate(cols, axis=1) + bias)
    s = jnp.concatenate(rows, axis=0)
    m = (u * s).astype(BF)
    o_ref[...] = x + _dot(m, wout_ref[...])


def _gmlp(h, g, w_in, v_norm, wmix, bmix, w_out):
    const = lambda i: (0, 0)
    kind = lambda i: (i // (N_TILES - 1), 0, 0, 0)
    return pl.pallas_call(
        _gmlp_kernel,
        grid=(N_TILES,),
        in_specs=[
            pl.BlockSpec((TM, D), lambda i: (i, 0)),
            pl.BlockSpec((1, D), const),
            pl.BlockSpec((D, 2 * D), const),
            pl.BlockSpec((1, D), const),
            pl.BlockSpec((None, NH, DH, DH), kind),
            pl.BlockSpec((None, DH, D), lambda i: (i // (N_TILES - 1), 0, 0)),
            pl.BlockSpec((D, D), const),
        ],
        out_specs=[
            pl.BlockSpec((TM, D), lambda i: (i, 0)),
            pl.BlockSpec((NS, D), const),
        ],
        out_shape=[
            jax.ShapeDtypeStruct((N, D), F32),
            jax.ShapeDtypeStruct((NS, D), F32),
        ],
        compiler_params=_params("arbitrary"),
        name="gmlp",
    )(h, g, w_in, v_norm, wmix, bmix, w_out)


def _ple_kernel(x_ref, g_ref, wg_ref, pe_ref, wp_ref, o_ref):
    x = x_ref[...]
    xn = _rms(x, g_ref[...]).astype(BF)
    gate = jax.nn.sigmoid(_dot(xn, wg_ref[...]))
    proj = _dot(pe_ref[...].astype(BF), wp_ref[...])
    o_ref[...] = x + gate * proj


def _ple(h, g, w_gate, pe, layer, w_proj):
    const = lambda i: (0, 0)
    return pl.pallas_call(
        _ple_kernel,
        grid=(N_TILES,),
        in_specs=[
            pl.BlockSpec((TM, D), lambda i: (i, 0)),
            pl.BlockSpec((1, D), const),
            pl.BlockSpec((D, D), const),
            pl.BlockSpec((None, TM, PLE), lambda i: (layer, i, 0)),
            pl.BlockSpec((PLE, D), const),
        ],
        out_specs=pl.BlockSpec((TM, D), lambda i: (i, 0)),
        out_shape=jax.ShapeDtypeStruct((N, D), F32),
        compiler_params=_params("parallel"),
        name="ple",
    )(h, g, w_gate, pe, w_proj)


def _kv_kernel(x_ref, g_ref, wk_ref, wv_ref, wvt_ref, wf_ref, bf_ref, kn_ref,
               k32_ref, v32_ref, kb_ref, vt_ref, a_ref, lf_ref, cs_ref, carry_ref):
    i = pl.program_id(0)
    is_sample = i == N_TILES - 1
    xs = _rms(x_ref[...], g_ref[...]).astype(BF)
    k = _head_rms(_dot(xs, wk_ref[...]), kn_ref[...])
    k32_ref[...] = k
    kb_ref[...] = k.astype(BF)
    v32_ref[...] = _dot(xs, wv_ref[...])
    vt_ref[...] = _dot_nt(wvt_ref[...], xs).astype(BF)

    f = _dot_nt(wf_ref[...], xs)[:NH] + bf_ref[...]
    lf = jnp.minimum(f, 0.0) - jnp.log1p(jnp.exp(-jnp.abs(f)))
    lf_ref[...] = lf

    lane = lax.broadcasted_iota(jnp.int32, (NH, TM), 1)
    pos = lane & jnp.where(is_sample, DEC_T - 1, TM - 1)
    y = lf
    sh = 1
    while sh < TM:
        y = y + jnp.where(pos >= sh, pltpu.roll(y, sh, 1), 0.0)
        sh *= 2

    @pl.when(jnp.logical_or(i % (SEQ // TM) == 0, is_sample))
    def _():
        carry_ref[...] = jnp.zeros_like(carry_ref)

    y = y + carry_ref[:, 0:1]
    cs_ref[...] = y
    carry_ref[...] = jnp.broadcast_to(y[:, TM - 1:TM], carry_ref.shape)

    hi, mid, lo = _split3(-(y * LOG2E))
    row = lax.broadcasted_iota(jnp.int32, (NH, TM), 0)
    ones = jnp.where(row < 3, 1.0, 0.0)
    pad = jnp.zeros((DH - 4 * NH, TM), F32)
    a_ref[...] = jnp.transpose(jnp.concatenate([hi, mid, lo, ones, pad], axis=0)).astype(BF)


def _kv(h, g, wk, wv, wvt, wf, bf, kn):
    const = lambda i: (0, 0)
    tok = lambda i: (i, 0)
    lanes = lambda i: (0, i)
    return pl.pallas_call(
        _kv_kernel,
        grid=(N_TILES,),
        in_specs=[
            pl.BlockSpec((TM, D), tok),
            pl.BlockSpec((1, D), const),
            pl.BlockSpec((D, D), const),
            pl.BlockSpec((D, D), const),
            pl.BlockSpec((D, D), const),
            pl.BlockSpec((16, D), const),
            pl.BlockSpec((NH, 1), const),
            pl.BlockSpec((1, D), const),
        ],
        out_specs=[
            pl.BlockSpec((TM, D), tok),
            pl.BlockSpec((TM, D), tok),
            pl.BlockSpec((TM, D), tok),
            pl.BlockSpec((D, TM), lanes),
            pl.BlockSpec((TM, DH), tok),
            pl.BlockSpec((NH, TM), lanes),
            pl.BlockSpec((NH, TM), lanes),
        ],
        out_shape=[
            jax.ShapeDtypeStruct((N, D), F32),
            jax.ShapeDtypeStruct((N, D), F32),
            jax.ShapeDtypeStruct((N, D), BF),
            jax.ShapeDtypeStruct((D, N), BF),
            jax.ShapeDtypeStruct((N, DH), BF),
            jax.ShapeDtypeStruct((NH, N), F32),
            jax.ShapeDtypeStruct((NH, N), F32),
        ],
        scratch_shapes=[pltpu.VMEM((NH, 128), F32)],
        compiler_params=_params("arbitrary"),
        name="shared_kv",
    )(h, g, wk, wv, wvt, wf, bf, kn)


def _qproj_kernel(x_ref, g_ref, wq_ref, qn_ref, q_ref):
    xn = _rms(x_ref[...], g_ref[...]).astype(BF)
    q_ref[...] = _head_rms(_dot(xn, wq_ref[...]), qn_ref[...]) * QSCALE


def _qproj(h, g, wq, qn):
    const = lambda i: (0, 0)
    return pl.pallas_call(
        _qproj_kernel,
        grid=(N_TILES,),
        in_specs=[
            pl.BlockSpec((TM, D), lambda i: (i, 0)),
            pl.BlockSpec((1, D), const),
            pl.BlockSpec((D, D), const),
            pl.BlockSpec((1, D), const),
        ],
        out_specs=pl.BlockSpec((TM, D), lambda i: (i, 0)),
        out_shape=jax.ShapeDtypeStruct((N, D), F32),
        compiler_params=_params("parallel"),
        name="qproj",
    )(h, g, wq, qn)


def _oproj_kernel(x_ref, o_ref, wo_ref, y_ref):
    y_ref[...] = x_ref[...] + _dot(o_ref[...], wo_ref[...])


def _oproj(h, o, wo):
    return pl.pallas_call(
        _oproj_kernel,
        grid=(N_TILES,),
        in_specs=[
            pl.BlockSpec((TM, D), lambda i: (i, 0)),
            pl.BlockSpec((TM, D), lambda i: (i, 0)),
            pl.BlockSpec((D, D), lambda i: (0, 0)),
        ],
        out_specs=pl.BlockSpec((TM, D), lambda i: (i, 0)),
        out_shape=jax.ShapeDtypeStruct((N, D), F32),
        compiler_params=_params("parallel"),
        name="oproj",
    )(h, o, wo)


def _attn_prompt_kernel(q_ref, k_ref, a_ref, vt_ref, c_ref, o_ref, qa_ref, m_ref, l_ref, acc_ref):
    hb = pl.program_id(1)
    qi = pl.program_id(2)
    q0 = pl.multiple_of(qi * TQ, TQ)

    row = lax.broadcasted_iota(jnp.int32, (DH, TQ), 0)
    for u in range(HB):
        qa_ref[u, 0:DH, :] = jnp.transpose(q_ref[:, u * DH:(u + 1) * DH]).astype(BF)
        hi, mid, lo = _split3(c_ref[u, :, pl.ds(q0, TQ)] * LOG2E)
        aug = jnp.where(row == 3 * NH, hi, jnp.where(row == 3 * NH + 1, mid, jnp.where(row == 3 * NH + 2, lo, 0.0)))
        own = jnp.logical_and(row < 3 * NH, (row & (NH - 1)) == hb * HB + u)
        qa_ref[u, DH:2 * DH, :] = jnp.where(own, 1.0, aug).astype(BF)

    m_ref[...] = jnp.full_like(m_ref, -jnp.inf)
    l_ref[...] = jnp.zeros_like(l_ref)
    acc_ref[...] = jnp.zeros_like(acc_ref)

    def step(kj, diagonal):
        k0 = pl.multiple_of(kj * TQ, TQ)
        a = a_ref[pl.ds(k0, TQ), :]
        for u in range(HB):
            kaug = jnp.concatenate([k_ref[pl.ds(k0, TQ), u * DH:(u + 1) * DH], a], axis=1)
            s = _dot(kaug, qa_ref[u])
            if diagonal:
                key = lax.broadcasted_iota(jnp.int32, (TQ, TQ), 0)
                qry = lax.broadcasted_iota(jnp.int32, (TQ, TQ), 1)
                s = jnp.where(key <= qry, s, -jnp.inf)
            m_prev = m_ref[u]
            m_new = jnp.maximum(m_prev, jnp.max(s, axis=0, keepdims=True))
            alpha = jnp.exp2(m_prev - m_new)
            p = jnp.exp2(s - m_new)
            l_ref[u] = alpha * l_ref[u] + jnp.sum(p, axis=0, keepdims=True)
            acc_ref[u] = alpha * acc_ref[u] + _dot(vt_ref[u * DH:(u + 1) * DH, pl.ds(k0, TQ)], p.astype(BF))
            m_ref[u] = m_new

    def body(kj, carry):
        step(kj, False)
        return carry

    lax.fori_loop(0, qi, body, 0)
    step(qi, True)
    for u in range(HB):
        o_ref[:, u * DH:(u + 1) * DH] = jnp.transpose(acc_ref[u] / l_ref[u]).astype(BF)


def _attn_prompt(q, kb, a, vt, cs3):
    nq = SEQ // TQ
    return pl.pallas_call(
        _attn_prompt_kernel,
        grid=(BATCH, NH // HB, nq),
        in_specs=[
            pl.BlockSpec((TQ, HB * DH), lambda b, h, qi: (b * nq + qi, h)),
            pl.BlockSpec((SEQ, HB * DH), lambda b, h, qi: (b, h)),
            pl.BlockSpec((SEQ, DH), lambda b, h, qi: (b, 0)),
            pl.BlockSpec((HB * DH, SEQ), lambda b, h, qi: (h, b)),
            pl.BlockSpec((HB, 1, SEQ), lambda b, h, qi: (h, 0, b)),
        ],
        out_specs=pl.BlockSpec((TQ, HB * DH), lambda b, h, qi: (b * nq + qi, h)),
        out_shape=jax.ShapeDtypeStruct((NP, D), BF),
        scratch_shapes=[
            pltpu.VMEM((HB, 2 * DH, TQ), BF),
            pltpu.VMEM((HB, 1, TQ), F32),
            pltpu.VMEM((HB, 1, TQ), F32),
            pltpu.VMEM((HB, DH, TQ), F32),
        ],
        compiler_params=_params("parallel", "parallel", "arbitrary"),
        name="attn_prompt",
    )(q, kb, a, vt, cs3)


def _suffix_kernel(x_ref, excl_ref, tot_ref):
    x = x_ref[...]
    width = PAGE * NH
    lane = lax.broadcasted_iota(jnp.int32, x.shape, 1)
    y = x
    t = x
    sh = NH
    while sh < width:
        y = y + jnp.where(lane < width - sh, pltpu.roll(y, width - sh, 1), 0.0)
        t = t + pltpu.roll(t, sh, 1)
        sh *= 2
    excl_ref[...] = y - x
    tot_ref[...] = t


def _suffix(lf_rows):
    rows, width = lf_rows.shape
    tr = 256
    spec = pl.BlockSpec((tr, width), lambda i: (i, 0))
    return pl.pallas_call(
        _suffix_kernel,
        grid=(rows // tr,),
        in_specs=[spec],
        out_specs=[spec, spec],
        out_shape=[jax.ShapeDtypeStruct((rows, width), F32)] * 2,
        compiler_params=_params("parallel"),
        name="cache_suffix",
    )(lf_rows)


def _attn_sample_kernel(pt_ref, q_ref, kn_ref, vn_ref, cq_ref, ckn_ref, *rest):
    pp = PAGES_PER_STEP
    k_refs, v_refs = rest[:pp], rest[pp:2 * pp]
    e_refs, t_refs = rest[2 * pp:3 * pp], rest[3 * pp:4 * pp]
    o_ref, m_ref, l_ref, acc_ref, carry_ref = rest[4 * pp:]
    p = pl.program_id(1)
    q = q_ref[0]
    cq = cq_ref[0][:, 0:1] * LOG2E

    @pl.when(p == 0)
    def _():
        s = _dot_nt(q, kn_ref[0])
        s = s + cq - ckn_ref[0] * LOG2E
        row = lax.broadcasted_iota(jnp.int32, s.shape, 0)
        col = lax.broadcasted_iota(jnp.int32, s.shape, 1)
        ok = jnp.logical_and((col & (NH - 1)) == (row & (NH - 1)), (col >> 3) <= (row >> 3))
        s = jnp.where(ok, s, -jnp.inf)
        m = jnp.max(s, axis=1, keepdims=True)
        e = jnp.exp2(s - m)
        m_ref[...] = m
        l_ref[...] = jnp.sum(e, axis=1, keepdims=True)
        acc_ref[...] = _dot(e.astype(BF), vn_ref[0])
        carry_ref[...] = jnp.zeros_like(carry_ref)

    carry = carry_ref[...]
    tails = []
    for j in range(pp):
        tails.append(e_refs[j][0] + carry)
        carry = carry + t_refs[j][0]
    carry_ref[...] = carry
    tail = jnp.concatenate(tails, axis=1) * LOG2E
    k = jnp.concatenate([r[0].reshape(PAGE * NH, DH).astype(BF) for r in k_refs], axis=0)
    v = jnp.concatenate([r[0].reshape(PAGE * NH, DH).astype(BF) for r in v_refs], axis=0)
    s = _dot_nt(q, k)
    row = lax.broadcasted_iota(jnp.int32, s.shape, 0)
    col = lax.broadcasted_iota(jnp.int32, s.shape, 1)
    s = jnp.where((col & (NH - 1)) == (row & (NH - 1)), s + cq + tail, -jnp.inf)
    m_prev = m_ref[...]
    m_new = jnp.maximum(m_prev, jnp.max(s, axis=1, keepdims=True))
    alpha = jnp.exp2(m_prev - m_new)
    e = jnp.exp2(s - m_new)
    l_ref[...] = alpha * l_ref[...] + jnp.sum(e, axis=1, keepdims=True)
    acc_ref[...] = alpha * acc_ref[...] + _dot(e.astype(BF), v)
    m_ref[...] = m_new

    @pl.when(p == pl.num_programs(1) - 1)
    def _():
        o_ref[0] = acc_ref[...] / l_ref[...]


def _attn_sample(page_table, q, kn, vn, cq, ckn, cache_k, cache_v, excl, tot):
    pp = PAGES_PER_STEP

    def seq(b, p, pt):
        return (b, 0, 0)

    def page3(j):
        return lambda b, p, pt: (pt[b, N_PAGES - 1 - (pp * p + j)], 0, 0)

    def page4(j):
        return lambda b, p, pt: (pt[b, N_PAGES - 1 - (pp * p + j)], 0, 0, 0)

    in_specs = [
        pl.BlockSpec((1, ROWS, DH), seq),
        pl.BlockSpec((1, 128, DH), seq),
        pl.BlockSpec((1, 128, DH), seq),
        pl.BlockSpec((1, ROWS, 128), seq),
        pl.BlockSpec((1, 1, 128), seq),
    ]
    in_specs += [pl.BlockSpec((1, PAGE, NH, DH), page4(j)) for j in range(pp)]
    in_specs += [pl.BlockSpec((1, PAGE, NH, DH), page4(j)) for j in range(pp)]
    in_specs += [pl.BlockSpec((1, 1, PAGE * NH), page3(j)) for j in range(pp)]
    in_specs += [pl.BlockSpec((1, 1, PAGE * NH), page3(j)) for j in range(pp)]
    grid_spec = pltpu.PrefetchScalarGridSpec(
        num_scalar_prefetch=1,
        grid=(DEC_B, N_PAGES // pp),
        in_specs=in_specs,
        out_specs=pl.BlockSpec((1, ROWS, DH), seq),
        scratch_shapes=[
            pltpu.VMEM((ROWS, 1), F32),
            pltpu.VMEM((ROWS, 1), F32),
            pltpu.VMEM((ROWS, DH), F32),
            pltpu.VMEM((1, PAGE * NH), F32),
        ],
    )
    return pl.pallas_call(
        _attn_sample_kernel,
        grid_spec=grid_spec,
        out_shape=jax.ShapeDtypeStruct((DEC_B, ROWS, DH), F32),
        compiler_params=_params("parallel", "arbitrary"),
        name="attn_sample",
    )(page_table, q, kn, vn, cq, ckn,
      *([cache_k] * pp), *([cache_v] * pp), *([excl] * pp), *([tot] * pp))


def kernel(x_prompt, x_sample, p_prompt, p_sample, cache_k, cache_v, cache_logf, page_table,
           ffn1_norm, ffn1_w1, ffn1_w3, ffn1_w2, mix_norm, ffn2_norm, ffn2_w1, ffn2_w3, ffn2_w2,
           ple_norm, ple_w_gate, ple_w_proj, gmlp_w_in, gmlp_v_norm, gmlp_w_s, gmlp_b_s, gmlp_w_out,
           kv_norm, w_kvf, b_f, k_norm, att_w_q, q_norm, att_w_o):
    bf = lambda w: w.astype(BF)
    row = lambda g: g.reshape(1, -1)

    h = jnp.concatenate([x_prompt.reshape(NP, D), x_sample.reshape(NS, D)], axis=0)
    pe = jnp.concatenate([p_prompt.reshape(DEPTH, NP, PLE), p_sample.reshape(DEPTH, NS, PLE)], axis=1)

    tril = jnp.tril(jnp.ones((DH, DH), F32))
    w_prompt = gmlp_w_s * tril
    small = (gmlp_w_s[:, :, :DEC_T, :DEC_T] * tril[:DEC_T, :DEC_T])
    eye = jnp.eye(DH // DEC_T, dtype=F32)
    w_sample = jnp.einsum('ab,lgts->lgatbs', eye, small).reshape(N_A, NH, DH, DH)
    wmix = bf(jnp.stack([w_prompt, w_sample], axis=1))
    b_prompt = jnp.repeat(gmlp_b_s.transpose(0, 2, 1), DH, axis=2)
    b_sample = jnp.tile(b_prompt[:, :DEC_T], (1, DH // DEC_T, 1))
    bmix = jnp.stack([b_prompt, b_sample], axis=1)

    wk = bf(w_kvf[:, :D])
    wv = bf(w_kvf[:, D:2 * D])
    wf = bf(jnp.pad(w_kvf[:, 2 * D:].T, ((0, 16 - NH), (0, 0))))

    kv = None
    gmlp_vs = []
    for i in range(DEPTH):
        h = _ffn(h, row(ffn1_norm[i]), bf(ffn1_w1[i]), bf(ffn1_w3[i]), bf(ffn1_w2[i]))
        if i < N_A:
            h, v_rows = _gmlp(h, row(mix_norm[i]), bf(gmlp_w_in[i]), row(gmlp_v_norm[i]),
                              wmix[i], bmix[i], bf(gmlp_w_out[i]))
            gmlp_vs.append(v_rows.reshape(DEC_B, DEC_T, D))
        else:
            j = i - N_A
            k32, v32, kb, vt, a, lft, cst, excl, tot = kv
            q = _qproj(h, row(mix_norm[i]), bf(att_w_q[j]), row(jnp.tile(q_norm[j], NH)))
            o_p = _attn_prompt(q, kb, a, vt, cst.reshape(NH, 1, N))
            heads = lambda x: x[NP:].reshape(DEC_B, ROWS, DH)
            pad_rows = lambda x: jnp.pad(x, ((0, 0), (0, 128 - ROWS), (0, 0)))
            c_new = cst[:, NP:].reshape(NH, DEC_B, DEC_T).transpose(1, 2, 0).reshape(DEC_B, ROWS)
            cq = jnp.broadcast_to(c_new[:, :, None], (DEC_B, ROWS, 128))
            ckn = jnp.pad(c_new, ((0, 0), (0, 128 - ROWS)))[:, None, :]
            o_s = _attn_sample(page_table, bf(heads(q)), pad_rows(heads(kb)), pad_rows(bf(heads(v32))),
                               cq, ckn, cache_k, cache_v, excl, tot)
            o = jnp.concatenate([o_p, o_s.reshape(NS, D).astype(BF)], axis=0)
            h = _oproj(h, o, bf(att_w_o[j]))
        h = _ffn(h, row(ffn2_norm[i]), bf(ffn2_w1[i]), bf(ffn2_w3[i]), bf(ffn2_w2[i]))
        h = _ple(h, row(ple_norm[i]), bf(ple_w_gate[i]), pe, i, bf(ple_w_proj[i]))
        if i == N_A - 1:
            k32, v32, kb, vt, a, lft, cst = _kv(h, row(kv_norm), wk, wv, bf(w_kvf[:, D:2 * D].T), wf,
                                                b_f.reshape(NH, 1), row(jnp.tile(k_norm, NH)))
            n_pool = cache_logf.shape[0]
            excl, tot = _suffix(cache_logf.reshape(n_pool, PAGE * NH))
            kv = (k32, v32, kb, vt, a, lft, cst,
                  excl.reshape(n_pool, 1, PAGE * NH), tot.reshape(n_pool, 1, PAGE * NH))

    k32, v32, kb, vt, a, lft, cst, excl, tot = kv
    lf_tok = lft.T
    return (
        h[:NP].reshape(BATCH, SEQ, D),
        h[NP:].reshape(DEC_B, DEC_T, D),
        k32[:NP].reshape(BATCH, SEQ, NH, DH),
        v32[:NP].reshape(BATCH, SEQ, NH, DH),
        lf_tok[:NP].reshape(BATCH, SEQ, NH),
        k32[NP:].reshape(DEC_B, DEC_T, NH, DH),
        v32[NP:].reshape(DEC_B, DEC_T, NH, DH),
        lf_tok[NP:].reshape(DEC_B, DEC_T, NH),
        jnp.stack(gmlp_vs),
    )
```

```python
import jax
import jax.numpy as jnp
from jax import lax
from jax.experimental import pallas as pl
from jax.experimental.pallas import tpu as pltpu

D = 1024
DFF = 2816
NH = 8
DH = 128
PLE = 256
DEPTH = 4
N_A = 2
BATCH, SEQ = 4, 4096
DEC_B, DEC_T = 128, 4
PAGE = 128
N_PAGES = 16
NP = BATCH * SEQ
NS = DEC_B * DEC_T
N = NP + NS
EPS = 1e-6
LOG2E = 1.4426950408889634
QSCALE = DH ** -0.5 * LOG2E

BF = jnp.bfloat16
F32 = jnp.float32

TM = 512
NPT = NP // TM
N_TILES = NPT + 1
FFN_TM = 768
FFN_TF = 768
FFN_EDGE = 256
TQ = 512
PAGES_PER_STEP = 8
ROWS = DEC_T * NH
VMEM_LIMIT = 56 * 1024 * 1024

NT_DIMS = (((1,), (1,)), ((), ()))


def _params(*sem):
    return pltpu.CompilerParams(dimension_semantics=sem, vmem_limit_bytes=VMEM_LIMIT)


def _resident(shape):
    return pl.BlockSpec(shape, lambda *_: (0,) * len(shape), pipeline_mode=pl.Buffered(1))


def _tok_specs(width):
    return [pl.BlockSpec((TM, width), lambda i: (jnp.minimum(i, NPT - 1), 0)),
            pl.BlockSpec((NS, width), lambda i: (0, 0))]


def _tok_shapes(width, dtype):
    return [jax.ShapeDtypeStruct((NP, width), dtype), jax.ShapeDtypeStruct((NS, width), dtype)]


def _load_tok(p_ref, s_ref):
    return jnp.where(pl.program_id(0) == NPT, s_ref[...], p_ref[...])


def _store_tok(p_ref, s_ref, y):
    i = pl.program_id(0)

    @pl.when(i < NPT)
    def _():
        p_ref[...] = y

    @pl.when(i == NPT)
    def _():
        s_ref[...] = y


def _rms(x, g):
    ms = jnp.mean(x * x, axis=-1, keepdims=True)
    return x * lax.rsqrt(ms + EPS) * g


def _head_rms(x, g):
    outs = []
    for h in range(NH):
        xh = x[:, h * DH:(h + 1) * DH]
        ms = jnp.mean(xh * xh, axis=-1, keepdims=True)
        outs.append(xh * lax.rsqrt(ms + EPS))
    return jnp.concatenate(outs, axis=-1) * g


def _dot(a, b):
    return jnp.dot(a, b, preferred_element_type=F32)


def _dot_nt(a, b):
    return lax.dot_general(a, b, NT_DIMS, preferred_element_type=F32)


def _split3(x):
    hi = x.astype(BF).astype(F32)
    r = x - hi
    mid = r.astype(BF).astype(F32)
    return hi, mid, r - mid


def _ffn_kernel(x_ref, g_ref, w1_ref, w3_ref, w2_ref, o_ref, xn_ref, acc_ref):
    def chunk(f0, width):
        xn = xn_ref[...]
        h1 = _dot(xn, w1_ref[:, pl.ds(f0, width)])
        h3 = _dot(xn, w3_ref[:, pl.ds(f0, width)])
        a = (h1 * jax.nn.sigmoid(h1) * h3).astype(BF)
        return _dot(a, w2_ref[pl.ds(f0, width), :])

    xn_ref[...] = _rms(x_ref[...], g_ref[...]).astype(BF)
    acc_ref[...] = chunk(0, FFN_EDGE)

    for f0 in range(FFN_EDGE, DFF - FFN_EDGE, FFN_TF):
        acc_ref[...] += chunk(f0, FFN_TF)
    o_ref[...] = x_ref[...] + 0.5 * (acc_ref[...] + chunk(DFF - FFN_EDGE, FFN_EDGE))


def _ffn(h, g, w1, w3, w2):
    tm = FFN_TM
    return pl.pallas_call(
        _ffn_kernel,
        grid=(N // tm,),
        in_specs=[pl.BlockSpec((tm, D), lambda i: (i, 0)),
                  _resident((1, D)), _resident((D, DFF)), _resident((D, DFF)), _resident((DFF, D))],
        out_specs=pl.BlockSpec((tm, D), lambda i: (i, 0)),
        out_shape=jax.ShapeDtypeStruct((N, D), F32),
        scratch_shapes=[pltpu.VMEM((tm, D), BF), pltpu.VMEM((tm, D), F32)],
        compiler_params=_params("parallel"),
        name="ffn",
    )(h, g, w1, w3, w2)


def _gmlp_kernel(x_ref, g_ref, win_ref, vn_ref, wmix_ref, bmix_ref, wout_ref, o_ref, vs_ref):
    x = x_ref[...]
    xn = _rms(x, g_ref[...]).astype(BF)
    uv = _dot(xn, win_ref[...])
    u = uv[:, :D]
    v = _rms(uv[:, D:], vn_ref[...])

    @pl.when(pl.program_id(0) == NPT)
    def _():
        vs_ref[...] = v

    vb = v.astype(BF)
    bias = bmix_ref[...]
    rows = []
    for c in range(TM // DH):
        cols = []
        for g in range(NH):
            blk = vb[c * DH:(c + 1) * DH, g * DH:(g + 1) * DH]
            cols.append(_dot(wmix_ref[g], blk))
        rows.append(jnp.concatenate(cols, axis=1) + bias)
    s = jnp.concatenate(rows, axis=0)
    m = (u * s).astype(BF)
    o_ref[...] = x + _dot(m, wout_ref[...])


def _gmlp(h, g, w_in, v_norm, wmix, bmix, w_out):
    const = lambda i: (0, 0)
    tok = pl.BlockSpec((TM, D), lambda i: (i, 0))
    return pl.pallas_call(
        _gmlp_kernel,
        grid=(N_TILES,),
        in_specs=[
            tok,
            pl.BlockSpec((1, D), const),
            pl.BlockSpec((D, 2 * D), const),
            pl.BlockSpec((1, D), const),
            pl.BlockSpec((None, NH, DH, DH), lambda i: (i // NPT, 0, 0, 0)),
            pl.BlockSpec((None, DH, D), lambda i: (i // NPT, 0, 0)),
            pl.BlockSpec((D, D), const),
        ],
        out_specs=[tok, pl.BlockSpec((NS, D), const)],
        out_shape=[jax.ShapeDtypeStruct((N, D), F32), jax.ShapeDtypeStruct((NS, D), F32)],
        compiler_params=_params("arbitrary"),
        name="gmlp",
    )(h, g, w_in, v_norm, wmix, bmix, w_out)


def _ple_kernel(x_ref, g_ref, wg_ref, pep_ref, pes_ref, wp_ref, *o_refs):
    x = x_ref[...]
    xn = _rms(x, g_ref[...]).astype(BF)
    gate = jax.nn.sigmoid(_dot(xn, wg_ref[...]))
    proj = _dot(_load_tok(pep_ref, pes_ref).astype(BF), wp_ref[...])
    y = x + gate * proj
    if len(o_refs) == 1:
        o_refs[0][...] = y
    else:
        _store_tok(*o_refs, y)


def _ple(h, g, w_gate, pe_p, pe_s, layer, w_proj, split_out):
    const = lambda i: (0, 0)
    tok = pl.BlockSpec((TM, D), lambda i: (i, 0))
    return pl.pallas_call(
        _ple_kernel,
        grid=(N_TILES,),
        in_specs=[
            tok,
            pl.BlockSpec((1, D), const),
            pl.BlockSpec((D, D), const),
            pl.BlockSpec((None, TM, PLE), lambda i: (layer, jnp.minimum(i, NPT - 1), 0)),
            pl.BlockSpec((None, NS, PLE), lambda i: (layer, 0, 0)),
            pl.BlockSpec((PLE, D), const),
        ],
        out_specs=_tok_specs(D) if split_out else tok,
        out_shape=_tok_shapes(D, F32) if split_out else jax.ShapeDtypeStruct((N, D), F32),
        compiler_params=_params("arbitrary"),
        name="ple",
    )(h, g, w_gate, pe_p, pe_s, w_proj)


def _kv_kernel(x_ref, g_ref, wk_ref, wv_ref, wvt_ref, wf_ref, bf_ref, kn_ref,
               k32p_ref, k32s_ref, v32p_ref, v32s_ref, kb_ref, vt_ref, a_ref, lf_ref, cs_ref, carry_ref):
    i = pl.program_id(0)
    is_sample = i == NPT
    xs = _rms(x_ref[...], g_ref[...]).astype(BF)
    k = _head_rms(_dot(xs, wk_ref[...]), kn_ref[...])
    _store_tok(k32p_ref, k32s_ref, k)
    _store_tok(v32p_ref, v32s_ref, _dot(xs, wv_ref[...]))

    f = _dot_nt(wf_ref[...], xs)[:NH] + bf_ref[...]
    lf = jnp.minimum(f, 0.0) - jnp.log1p(jnp.exp(-jnp.abs(f)))
    lf_ref[...] = lf

    lane = lax.broadcasted_iota(jnp.int32, (NH, TM), 1)
    pos = lane & jnp.where(is_sample, DEC_T - 1, TM - 1)
    y = lf
    sh = 1
    while sh < TM:
        y = y + jnp.where(pos >= sh, pltpu.roll(y, sh, 1), 0.0)
        sh *= 2

    @pl.when(jnp.logical_or(i % (SEQ // TM) == 0, is_sample))
    def _():
        carry_ref[...] = jnp.zeros_like(carry_ref)

    y = y + carry_ref[:, 0:1]
    cs_ref[...] = y
    carry_ref[...] = jnp.broadcast_to(y[:, TM - 1:TM], carry_ref.shape)

    @pl.when(i < NPT)
    def _():
        kb_ref[...] = k.astype(BF)
        vt_ref[...] = _dot_nt(wvt_ref[...], xs).astype(BF)
        hi, mid, lo = _split3(-(y * LOG2E))
        row = lax.broadcasted_iota(jnp.int32, (NH, TM), 0)
        ones = jnp.where(row < 3, 1.0, 0.0)
        pad = jnp.zeros((DH - 4 * NH, TM), F32)
        a_ref[...] = jnp.transpose(jnp.concatenate([hi, mid, lo, ones, pad], axis=0)).astype(BF)


def _kv(h, g, wk, wv, wvt, wf, bf, kn):
    const = lambda i: (0, 0)
    ptile = lambda i: (jnp.minimum(i, NPT - 1), 0)
    return pl.pallas_call(
        _kv_kernel,
        grid=(N_TILES,),
        in_specs=[
            pl.BlockSpec((TM, D), lambda i: (i, 0)),
            pl.BlockSpec((1, D), const),
            pl.BlockSpec((D, D), const),
            pl.BlockSpec((D, D), const),
            pl.BlockSpec((D, D), const),
            pl.BlockSpec((16, D), const),
            pl.BlockSpec((NH, 1), const),
            pl.BlockSpec((1, D), const),
        ],
        out_specs=_tok_specs(D) + _tok_specs(D) + [
            pl.BlockSpec((TM, D), ptile),
            pl.BlockSpec((D, TM), lambda i: (0, jnp.minimum(i, NPT - 1))),
            pl.BlockSpec((TM, DH), ptile),
            pl.BlockSpec((NH, TM), lambda i: (0, i)),
            pl.BlockSpec((NH, TM), lambda i: (0, i)),
        ],
        out_shape=_tok_shapes(D, F32) + _tok_shapes(D, F32) + [
            jax.ShapeDtypeStruct((NP, D), BF),
            jax.ShapeDtypeStruct((D, NP), BF),
            jax.ShapeDtypeStruct((NP, DH), BF),
            jax.ShapeDtypeStruct((NH, N), F32),
            jax.ShapeDtypeStruct((NH, N), F32),
        ],
        scratch_shapes=[pltpu.VMEM((NH, 128), F32)],
        compiler_params=_params("arbitrary"),
        name="shared_kv",
    )(h, g, wk, wv, wvt, wf, bf, kn)


def _qproj_kernel(x_ref, g_ref, wq_ref, qn_ref, q_ref):
    xn = _rms(x_ref[...], g_ref[...]).astype(BF)
    q_ref[...] = _head_rms(_dot(xn, wq_ref[...]), qn_ref[...]) * QSCALE


def _qproj(h, g, wq, qn):
    const = lambda i: (0, 0)
    tok = pl.BlockSpec((TM, D), lambda i: (i, 0))
    return pl.pallas_call(
        _qproj_kernel,
        grid=(N_TILES,),
        in_specs=[tok, pl.BlockSpec((1, D), const), pl.BlockSpec((D, D), const), pl.BlockSpec((1, D), const)],
        out_specs=tok,
        out_shape=jax.ShapeDtypeStruct((N, D), F32),
        compiler_params=_params("parallel"),
        name="qproj",
    )(h, g, wq, qn)


def _oproj_kernel(x_ref, op_ref, os_ref, wo_ref, y_ref):
    o = jnp.where(pl.program_id(0) == NPT, os_ref[...].astype(BF), op_ref[...])
    y_ref[...] = x_ref[...] + _dot(o, wo_ref[...])


def _oproj(h, o_p, o_s, wo):
    tok = pl.BlockSpec((TM, D), lambda i: (i, 0))
    return pl.pallas_call(
        _oproj_kernel,
        grid=(N_TILES,),
        in_specs=[tok] + _tok_specs(D) + [pl.BlockSpec((D, D), lambda i: (0, 0))],
        out_specs=tok,
        out_shape=jax.ShapeDtypeStruct((N, D), F32),
        compiler_params=_params("arbitrary"),
        name="oproj",
    )(h, o_p, o_s, wo)


def _attn_prompt_kernel(q_ref, k_ref, a_ref, vt_ref, c_ref, o_ref,
                        qa_ref, s0_ref, s1_ref, p0_ref, p1_ref, al_ref, m_ref, l_ref, acc_ref):
    h = pl.program_id(1)
    qi = pl.program_id(2)
    q0 = pl.multiple_of(qi * TQ, TQ)
    s_refs = (s0_ref, s1_ref)
    p_refs = (p0_ref, p1_ref)

    qa_ref[0:DH, :] = jnp.transpose(q_ref[...]).astype(BF)
    hi, mid, lo = _split3(c_ref[0, :, pl.ds(q0, TQ)] * LOG2E)
    row = lax.broadcasted_iota(jnp.int32, (DH, TQ), 0)
    aug = jnp.where(row == 3 * NH, hi, jnp.where(row == 3 * NH + 1, mid, jnp.where(row == 3 * NH + 2, lo, 0.0)))
    own = jnp.logical_and(row < 3 * NH, (row & (NH - 1)) == h)
    qa_ref[DH:2 * DH, :] = jnp.where(own, 1.0, aug).astype(BF)

    m_ref[...] = jnp.full_like(m_ref, -jnp.inf)
    l_ref[...] = jnp.zeros_like(l_ref)
    acc_ref[...] = jnp.zeros_like(acc_ref)

    def scores(g, slot):
        k0 = pl.multiple_of(g * TQ, TQ)
        kaug = jnp.concatenate([k_ref[pl.ds(k0, TQ), :], a_ref[pl.ds(k0, TQ), :]], axis=1)
        s_refs[slot][...] = _dot(kaug, qa_ref[...])

    def softmax(slot, diagonal):
        s = s_refs[slot][...]
        if diagonal:
            key = lax.broadcasted_iota(jnp.int32, (TQ, TQ), 0)
            qry = lax.broadcasted_iota(jnp.int32, (TQ, TQ), 1)
            s = jnp.where(key <= qry, s, -jnp.inf)
        m_prev = m_ref[...]
        m_new = jnp.maximum(m_prev, jnp.max(s, axis=0, keepdims=True))
        alpha = jnp.exp2(m_prev - m_new)
        p = jnp.exp2(s - m_new)
        l_ref[...] = alpha * l_ref[...] + jnp.sum(p, axis=0, keepdims=True)
        m_ref[...] = m_new
        al_ref[slot] = alpha
        p_refs[slot][...] = p.astype(BF)

    def values(g, slot):
        k0 = pl.multiple_of(g * TQ, TQ)
        acc_ref[...] = al_ref[slot] * acc_ref[...] + _dot(vt_ref[:, pl.ds(k0, TQ)], p_refs[slot][...])

    def full_step(g, slot):
        values(g - 1, 1 - slot)
        softmax(slot, False)
        scores(g + 1, 1 - slot)

    scores(0, 0)

    @pl.when(qi == 0)
    def _():
        softmax(0, True)
        values(0, 0)

    @pl.when(qi > 0)
    def _():
        softmax(0, False)
        scores(1, 1)

        def pair(t, carry):
            g = 2 * t + 1
            full_step(g, 1)
            full_step(g + 1, 0)
            return carry

        lax.fori_loop(0, (qi - 1) // 2, pair, 0)

        @pl.when(qi % 2 == 0)
        def _():
            full_step(qi - 1, 1)
            values(qi - 1, 1)
            softmax(0, True)
            values(qi, 0)

        @pl.when(qi % 2 == 1)
        def _():
            values(qi - 1, 0)
            softmax(1, True)
            values(qi, 1)

    o_ref[...] = jnp.transpose(acc_ref[...] / l_ref[...]).astype(BF)


def _attn_prompt(q, kb, a, vt, cs3):
    nq = SEQ // TQ
    return pl.pallas_call(
        _attn_prompt_kernel,
        grid=(BATCH, NH, nq),
        in_specs=[
            pl.BlockSpec((TQ, DH), lambda b, h, qi: (b * nq + qi, h)),
            pl.BlockSpec((SEQ, DH), lambda b, h, qi: (b, h)),
            pl.BlockSpec((SEQ, DH), lambda b, h, qi: (b, 0)),
            pl.BlockSpec((DH, SEQ), lambda b, h, qi: (h, b)),
            pl.BlockSpec((1, 1, SEQ), lambda b, h, qi: (h, 0, b)),
        ],
        out_specs=pl.BlockSpec((TQ, DH), lambda b, h, qi: (b * nq + qi, h)),
        out_shape=jax.ShapeDtypeStruct((NP, D), BF),
        scratch_shapes=[
            pltpu.VMEM((2 * DH, TQ), BF),
            pltpu.VMEM((TQ, TQ), F32),
            pltpu.VMEM((TQ, TQ), F32),
            pltpu.VMEM((TQ, TQ), BF),
            pltpu.VMEM((TQ, TQ), BF),
            pltpu.VMEM((2, 1, TQ), F32),
            pltpu.VMEM((1, TQ), F32),
            pltpu.VMEM((1, TQ), F32),
            pltpu.VMEM((DH, TQ), F32),
        ],
        compiler_params=_params("parallel", "parallel", "arbitrary"),
        name="attn_prompt",
    )(q, kb, a, vt, cs3)


def _suffix_kernel(x_ref, excl_ref, tot_ref):
    x = x_ref[...]
    width = PAGE * NH
    lane = lax.broadcasted_iota(jnp.int32, x.shape, 1)
    y = x
    t = x
    sh = NH
    while sh < width:
        y = y + jnp.where(lane < width - sh, pltpu.roll(y, width - sh, 1), 0.0)
        t = t + pltpu.roll(t, sh, 1)
        sh *= 2
    excl_ref[...] = y - x
    tot_ref[...] = t


def _suffix(lf_rows):
    rows, width = lf_rows.shape
    tr = 256
    spec = pl.BlockSpec((tr, width), lambda i: (i, 0))
    return pl.pallas_call(
        _suffix_kernel,
        grid=(rows // tr,),
        in_specs=[spec],
        out_specs=[spec, spec],
        out_shape=[jax.ShapeDtypeStruct((rows, width), F32)] * 2,
        compiler_params=_params("parallel"),
        name="cache_suffix",
    )(lf_rows)


def _attn_sample_kernel(pt_ref, q_ref, kn_ref, vn_ref, cq_ref, ckn_ref, *rest):
    pp = PAGES_PER_STEP
    k_refs, v_refs = rest[:pp], rest[pp:2 * pp]
    e_refs, t_refs = rest[2 * pp:3 * pp], rest[3 * pp:4 * pp]
    o_ref, m_ref, l_ref, acc_ref, carry_ref = rest[4 * pp:]
    p = pl.program_id(1)
    q = q_ref[0]
    cq = cq_ref[0][:, 0:1] * LOG2E

    @pl.when(p == 0)
    def _():
        s = _dot_nt(q, kn_ref[0])
        s = s + cq - ckn_ref[0] * LOG2E
        row = lax.broadcasted_iota(jnp.int32, s.shape, 0)
        col = lax.broadcasted_iota(jnp.int32, s.shape, 1)
        ok = jnp.logical_and((col & (NH - 1)) == (row & (NH - 1)), (col >> 3) <= (row >> 3))
        s = jnp.where(ok, s, -jnp.inf)
        m = jnp.max(s, axis=1, keepdims=True)
        e = jnp.exp2(s - m)
        m_ref[...] = m
        l_ref[...] = jnp.sum(e, axis=1, keepdims=True)
        acc_ref[...] = _dot(e.astype(BF), vn_ref[0])
        carry_ref[...] = jnp.zeros_like(carry_ref)

    carry = carry_ref[...]
    tails = []
    for j in range(pp):
        tails.append(e_refs[j][0] + carry)
        carry = carry + t_refs[j][0]
    carry_ref[...] = carry
    tail = jnp.concatenate(tails, axis=1) * LOG2E
    k = jnp.concatenate([r[0].reshape(PAGE * NH, DH).astype(BF) for r in k_refs], axis=0)
    v = jnp.concatenate([r[0].reshape(PAGE * NH, DH).astype(BF) for r in v_refs], axis=0)
    s = _dot_nt(q, k)
    row = lax.broadcasted_iota(jnp.int32, s.shape, 0)
    col = lax.broadcasted_iota(jnp.int32, s.shape, 1)
    s = jnp.where((col & (NH - 1)) == (row & (NH - 1)), s + cq + tail, -jnp.inf)
    m_prev = m_ref[...]
    m_new = jnp.maximum(m_prev, jnp.max(s, axis=1, keepdims=True))
    alpha = jnp.exp2(m_prev - m_new)
    e = jnp.exp2(s - m_new)
    l_ref[...] = alpha * l_ref[...] + jnp.sum(e, axis=1, keepdims=True)
    acc_ref[...] = alpha * acc_ref[...] + _dot(e.astype(BF), v)
    m_ref[...] = m_new

    @pl.when(p == pl.num_programs(1) - 1)
    def _():
        o_ref[0] = acc_ref[...] / l_ref[...]


def _attn_sample(page_table, q, kn, vn, cq, ckn, cache_k, cache_v, excl, tot):
    pp = PAGES_PER_STEP

    def seq(b, p, pt):
        return (b, 0, 0)

    def page3(j):
        return lambda b, p, pt: (pt[b, N_PAGES - 1 - (pp * p + j)], 0, 0)

    def page4(j):
        return lambda b, p, pt: (pt[b, N_PAGES - 1 - (pp * p + j)], 0, 0, 0)

    in_specs = [
        pl.BlockSpec((1, ROWS, DH), seq),
        pl.BlockSpec((1, 128, DH), seq),
        pl.BlockSpec((1, 128, DH), seq),
        pl.BlockSpec((1, ROWS, 128), seq),
        pl.BlockSpec((1, 1, 128), seq),
    ]
    in_specs += [pl.BlockSpec((1, PAGE, NH, DH), page4(j)) for j in range(pp)]
    in_specs += [pl.BlockSpec((1, PAGE, NH, DH), page4(j)) for j in range(pp)]
    in_specs += [pl.BlockSpec((1, 1, PAGE * NH), page3(j)) for j in range(pp)]
    in_specs += [pl.BlockSpec((1, 1, PAGE * NH), page3(j)) for j in range(pp)]
    grid_spec = pltpu.PrefetchScalarGridSpec(
        num_scalar_prefetch=1,
        grid=(DEC_B, N_PAGES // pp),
        in_specs=in_specs,
        out_specs=pl.BlockSpec((1, ROWS, DH), seq),
        scratch_shapes=[
            pltpu.VMEM((ROWS, 1), F32),
            pltpu.VMEM((ROWS, 1), F32),
            pltpu.VMEM((ROWS, DH), F32),
            pltpu.VMEM((1, PAGE * NH), F32),
        ],
    )
    return pl.pallas_call(
        _attn_sample_kernel,
        grid_spec=grid_spec,
        out_shape=jax.ShapeDtypeStruct((DEC_B, ROWS, DH), F32),
        compiler_params=_params("parallel", "arbitrary"),
        name="attn_sample",
    )(page_table, q, kn, vn, cq, ckn,
      *([cache_k] * pp), *([cache_v] * pp), *([excl] * pp), *([tot] * pp))


def kernel(x_prompt, x_sample, p_prompt, p_sample, cache_k, cache_v, cache_logf, page_table,
           ffn1_norm, ffn1_w1, ffn1_w3, ffn1_w2, mix_norm, ffn2_norm, ffn2_w1, ffn2_w3, ffn2_w2,
           ple_norm, ple_w_gate, ple_w_proj, gmlp_w_in, gmlp_v_norm, gmlp_w_s, gmlp_b_s, gmlp_w_out,
           kv_norm, w_kvf, b_f, k_norm, att_w_q, q_norm, att_w_o):
    bf = lambda w: w.astype(BF)
    row = lambda g: g.reshape(1, -1)

    h = jnp.concatenate([x_prompt.reshape(NP, D), x_sample.reshape(NS, D)], axis=0)
    pe_p, pe_s = p_prompt.reshape(DEPTH, NP, PLE), p_sample.reshape(DEPTH, NS, PLE)

    tril = jnp.tril(jnp.ones((DH, DH), F32))
    w_prompt = gmlp_w_s * tril
    small = (gmlp_w_s[:, :, :DEC_T, :DEC_T] * tril[:DEC_T, :DEC_T])
    eye = jnp.eye(DH // DEC_T, dtype=F32)
    w_sample = jnp.einsum('ab,lgts->lgatbs', eye, small).reshape(N_A, NH, DH, DH)
    wmix = bf(jnp.stack([w_prompt, w_sample], axis=1))
    b_prompt = jnp.repeat(gmlp_b_s.transpose(0, 2, 1), DH, axis=2)
    b_sample = jnp.tile(b_prompt[:, :DEC_T], (1, DH // DEC_T, 1))
    bmix = jnp.stack([b_prompt, b_sample], axis=1)

    wk = bf(w_kvf[:, :D])
    wv = bf(w_kvf[:, D:2 * D])
    wf = bf(jnp.pad(w_kvf[:, 2 * D:].T, ((0, 16 - NH), (0, 0))))

    kv = None
    gmlp_vs = []
    for i in range(DEPTH):
        h = _ffn(h, row(ffn1_norm[i]), bf(ffn1_w1[i]), bf(ffn1_w3[i]), bf(ffn1_w2[i]))
        if i < N_A:
            h, v_rows = _gmlp(h, row(mix_norm[i]), bf(gmlp_w_in[i]), row(gmlp_v_norm[i]),
                              wmix[i], bmix[i], bf(gmlp_w_out[i]))
            gmlp_vs.append(v_rows.reshape(DEC_B, DEC_T, D))
        else:
            j = i - N_A
            k32s, v32s, kb, vt, a, cst, excl, tot = kv
            q = _qproj(h, row(mix_norm[i]), bf(att_w_q[j]), row(jnp.tile(q_norm[j], NH)))
            o_p = _attn_prompt(q, kb, a, vt, cst.reshape(NH, 1, N))
            heads = lambda x: bf(x).reshape(DEC_B, ROWS, DH)
            pad_rows = lambda x: jnp.pad(x, ((0, 0), (0, 128 - ROWS), (0, 0)))
            c_new = cst[:, NP:].reshape(NH, DEC_B, DEC_T).transpose(1, 2, 0).reshape(DEC_B, ROWS)
            cq = jnp.broadcast_to(c_new[:, :, None], (DEC_B, ROWS, 128))
            ckn = jnp.pad(c_new, ((0, 0), (0, 128 - ROWS)))[:, None, :]
            o_s = _attn_sample(page_table, heads(q[NP:]), pad_rows(heads(k32s)), pad_rows(heads(v32s)),
                               cq, ckn, cache_k, cache_v, excl, tot)
            h = _oproj(h, o_p, o_s.reshape(NS, D), bf(att_w_o[j]))
        h = _ffn(h, row(ffn2_norm[i]), bf(ffn2_w1[i]), bf(ffn2_w3[i]), bf(ffn2_w2[i]))
        h = _ple(h, row(ple_norm[i]), bf(ple_w_gate[i]), pe_p, pe_s, i, bf(ple_w_proj[i]),
                 split_out=i == DEPTH - 1)
        if i == N_A - 1:
            k32p, k32s, v32p, v32s, kb, vt, a, lft, cst = _kv(
                h, row(kv_norm), wk, wv, bf(w_kvf[:, D:2 * D].T), wf,
                b_f.reshape(NH, 1), row(jnp.tile(k_norm, NH)))
            n_pool = cache_logf.shape[0]
            excl, tot = _suffix(cache_logf.reshape(n_pool, PAGE * NH))
            kv = (k32s, v32s, kb, vt, a, cst,
                  excl.reshape(n_pool, 1, PAGE * NH), tot.reshape(n_pool, 1, PAGE * NH))

    hp, hs = h
    lf_tok = lft.T
    return (
        hp.reshape(BATCH, SEQ, D),
        hs.reshape(DEC_B, DEC_T, D),
        k32p.reshape(BATCH, SEQ, NH, DH),
        v32p.reshape(BATCH, SEQ, NH, DH),
        lf_tok[:NP].reshape(BATCH, SEQ, NH),
        k32s.reshape(DEC_B, DEC_T, NH, DH),
        v32s.reshape(DEC_B, DEC_T, NH, DH),
        lf_tok[NP:].reshape(DEC_B, DEC_T, NH),
        jnp.stack(gmlp_vs),
    )
```

```python
import jax
import jax.numpy as jnp
from jax import lax
from jax.experimental import pallas as pl
from jax.experimental.pallas import tpu as pltpu

D = 1024
DFF = 2816
NH = 8
DH = 128
PLE = 256
DEPTH = 4
N_A = 2
BATCH, SEQ = 4, 4096
DEC_B, DEC_T = 128, 4
PAGE = 128
N_PAGES = 16
NP = BATCH * SEQ
NS = DEC_B * DEC_T
N = NP + NS
EPS = 1e-6
LOG2E = 1.4426950408889634
QSCALE = DH ** -0.5 * LOG2E

BF = jnp.bfloat16
F32 = jnp.float32

TM = 512
NPT = NP // TM
N_TILES = NPT + 1
FFN_TM = 768
FFN_TF = 768
FFN_EDGE = 256
TQ = 512
PAGES_PER_STEP = 8
ROWS = DEC_T * NH
VMEM_LIMIT = 56 * 1024 * 1024

NT_DIMS = (((1,), (1,)), ((), ()))


def _params(*sem):
    return pltpu.CompilerParams(dimension_semantics=sem, vmem_limit_bytes=VMEM_LIMIT)


def _resident(shape):
    return pl.BlockSpec(shape, lambda *_: (0,) * len(shape), pipeline_mode=pl.Buffered(1))


def _tok_specs(width):
    return [pl.BlockSpec((TM, width), lambda i: (jnp.minimum(i, NPT - 1), 0)),
            pl.BlockSpec((NS, width), lambda i: (0, 0))]


def _tok_shapes(width, dtype):
    return [jax.ShapeDtypeStruct((NP, width), dtype), jax.ShapeDtypeStruct((NS, width), dtype)]


def _load_tok(p_ref, s_ref):
    return jnp.where(pl.program_id(0) == NPT, s_ref[...], p_ref[...])


def _store_tok(p_ref, s_ref, y):
    i = pl.program_id(0)

    @pl.when(i < NPT)
    def _():
        p_ref[...] = y

    @pl.when(i == NPT)
    def _():
        s_ref[...] = y


def _rms(x, g):
    ms = jnp.mean(x * x, axis=-1, keepdims=True)
    return x * lax.rsqrt(ms + EPS) * g


def _head_rms(x, g):
    outs = []
    for h in range(NH):
        xh = x[:, h * DH:(h + 1) * DH]
        ms = jnp.mean(xh * xh, axis=-1, keepdims=True)
        outs.append(xh * lax.rsqrt(ms + EPS))
    return jnp.concatenate(outs, axis=-1) * g


def _dot(a, b):
    return jnp.dot(a, b, preferred_element_type=F32)


def _dot_nt(a, b):
    return lax.dot_general(a, b, NT_DIMS, preferred_element_type=F32)


def _split3(x):
    hi = x.astype(BF).astype(F32)
    r = x - hi
    mid = r.astype(BF).astype(F32)
    return hi, mid, r - mid


def _ffn_kernel(x_ref, g_ref, w1_ref, w3_ref, w2_ref, o_ref, xn_ref, acc_ref):
    def chunk(f0, width):
        xn = xn_ref[...]
        h1 = _dot(xn, w1_ref[:, pl.ds(f0, width)])
        h3 = _dot(xn, w3_ref[:, pl.ds(f0, width)])
        a = (h1 * jax.nn.sigmoid(h1) * h3).astype(BF)
        return _dot(a, w2_ref[pl.ds(f0, width), :])

    xn_ref[...] = _rms(x_ref[...], g_ref[...]).astype(BF)
    acc_ref[...] = chunk(0, FFN_EDGE)

    for f0 in range(FFN_EDGE, DFF - FFN_EDGE, FFN_TF):
        acc_ref[...] += chunk(f0, FFN_TF)
    o_ref[...] = x_ref[...] + 0.5 * (acc_ref[...] + chunk(DFF - FFN_EDGE, FFN_EDGE))


def _ffn(h, g, w1, w3, w2):
    tm = FFN_TM
    return pl.pallas_call(
        _ffn_kernel,
        grid=(N // tm,),
        in_specs=[pl.BlockSpec((tm, D), lambda i: (i, 0)),
                  _resident((1, D)), _resident((D, DFF)), _resident((D, DFF)), _resident((DFF, D))],
        out_specs=pl.BlockSpec((tm, D), lambda i: (i, 0)),
        out_shape=jax.ShapeDtypeStruct((N, D), F32),
        scratch_shapes=[pltpu.VMEM((tm, D), BF), pltpu.VMEM((tm, D), F32)],
        compiler_params=_params("parallel"),
        name="ffn",
    )(h, g, w1, w3, w2)


def _gmlp_kernel(x_ref, g_ref, win_ref, vn_ref, wmix_ref, bmix_ref, wout_ref, o_ref, vs_ref):
    x = x_ref[...]
    xn = _rms(x, g_ref[...]).astype(BF)
    uv = _dot(xn, win_ref[...])
    u = uv[:, :D]
    v = _rms(uv[:, D:], vn_ref[...])

    @pl.when(pl.program_id(0) == NPT)
    def _():
        vs_ref[...] = v

    vb = v.astype(BF)
    bias = bmix_ref[...]
    rows = []
    for c in range(TM // DH):
        cols = []
        for g in range(NH):
            blk = vb[c * DH:(c + 1) * DH, g * DH:(g + 1) * DH]
            cols.append(_dot(wmix_ref[g], blk))
        rows.append(jnp.concatenate(cols, axis=1) + bias)
    s = jnp.concatenate(rows, axis=0)
    m = (u * s).astype(BF)
    o_ref[...] = x + _dot(m, wout_ref[...])


def _gmlp(h, g, w_in, v_norm, wmix, bmix, w_out):
    const = lambda i: (0, 0)
    tok = pl.BlockSpec((TM, D), lambda i: (i, 0))
    return pl.pallas_call(
        _gmlp_kernel,
        grid=(N_TILES,),
        in_specs=[
            tok,
            pl.BlockSpec((1, D), const),
            pl.BlockSpec((D, 2 * D), const),
            pl.BlockSpec((1, D), const),
            pl.BlockSpec((None, NH, DH, DH), lambda i: (i // NPT, 0, 0, 0)),
            pl.BlockSpec((None, DH, D), lambda i: (i // NPT, 0, 0)),
            pl.BlockSpec((D, D), const),
        ],
        out_specs=[tok, pl.BlockSpec((NS, D), const)],
        out_shape=[jax.ShapeDtypeStruct((N, D), F32), jax.ShapeDtypeStruct((NS, D), F32)],
        compiler_params=_params("arbitrary"),
        name="gmlp",
    )(h, g, w_in, v_norm, wmix, bmix, w_out)


def _ple_kernel(x_ref, g_ref, wg_ref, pep_ref, pes_ref, wp_ref, *o_refs):
    x = x_ref[...]
    xn = _rms(x, g_ref[...]).astype(BF)
    gate = jax.nn.sigmoid(_dot(xn, wg_ref[...]))
    proj = _dot(_load_tok(pep_ref, pes_ref).astype(BF), wp_ref[...])
    y = x + gate * proj
    if len(o_refs) == 1:
        o_refs[0][...] = y
    else:
        _store_tok(*o_refs, y)


def _ple(h, g, w_gate, pe_p, pe_s, layer, w_proj, split_out):
    const = lambda i: (0, 0)
    tok = pl.BlockSpec((TM, D), lambda i: (i, 0))
    return pl.pallas_call(
        _ple_kernel,
        grid=(N_TILES,),
        in_specs=[
            tok,
            pl.BlockSpec((1, D), const),
            pl.BlockSpec((D, D), const),
            pl.BlockSpec((None, TM, PLE), lambda i: (layer, jnp.minimum(i, NPT - 1), 0)),
            pl.BlockSpec((None, NS, PLE), lambda i: (layer, 0, 0)),
            pl.BlockSpec((PLE, D), const),
        ],
        out_specs=_tok_specs(D) if split_out else tok,
        out_shape=_tok_shapes(D, F32) if split_out else jax.ShapeDtypeStruct((N, D), F32),
        compiler_params=_params("arbitrary"),
        name="ple",
    )(h, g, w_gate, pe_p, pe_s, w_proj)


def _kv_kernel(x_ref, g_ref, wk_ref, wv_ref, wvt_ref, wf_ref, bf_ref, kn_ref,
               k32p_ref, k32s_ref, v32p_ref, v32s_ref, kb_ref, vt_ref, a_ref, lf_ref, cs_ref, carry_ref):
    i = pl.program_id(0)
    is_sample = i == NPT
    xs = _rms(x_ref[...], g_ref[...]).astype(BF)
    k = _head_rms(_dot(xs, wk_ref[...]), kn_ref[...])
    _store_tok(k32p_ref, k32s_ref, k)
    _store_tok(v32p_ref, v32s_ref, _dot(xs, wv_ref[...]))

    f = _dot_nt(wf_ref[...], xs)[:NH] + bf_ref[...]
    lf = jnp.minimum(f, 0.0) - jnp.log1p(jnp.exp(-jnp.abs(f)))
    lf_ref[...] = lf

    lane = lax.broadcasted_iota(jnp.int32, (NH, TM), 1)
    pos = lane & jnp.where(is_sample, DEC_T - 1, TM - 1)
    y = lf
    sh = 1
    while sh < TM:
        y = y + jnp.where(pos >= sh, pltpu.roll(y, sh, 1), 0.0)
        sh *= 2

    @pl.when(jnp.logical_or(i % (SEQ // TM) == 0, is_sample))
    def _():
        carry_ref[...] = jnp.zeros_like(carry_ref)

    y = y + carry_ref[:, 0:1]
    cs_ref[...] = y
    carry_ref[...] = jnp.broadcast_to(y[:, TM - 1:TM], carry_ref.shape)

    @pl.when(i < NPT)
    def _():
        kb_ref[...] = k.astype(BF)
        vt_ref[...] = _dot_nt(wvt_ref[...], xs).astype(BF)
        hi, mid, lo = _split3(-(y * LOG2E))
        row = lax.broadcasted_iota(jnp.int32, (NH, TM), 0)
        ones = jnp.where(row < 3, 1.0, 0.0)
        pad = jnp.zeros((DH - 4 * NH, TM), F32)
        a_ref[...] = jnp.transpose(jnp.concatenate([hi, mid, lo, ones, pad], axis=0)).astype(BF)


def _kv(h, g, wk, wv, wvt, wf, bf, kn):
    const = lambda i: (0, 0)
    ptile = lambda i: (jnp.minimum(i, NPT - 1), 0)
    return pl.pallas_call(
        _kv_kernel,
        grid=(N_TILES,),
        in_specs=[
            pl.BlockSpec((TM, D), lambda i: (i, 0)),
            pl.BlockSpec((1, D), const),
            pl.BlockSpec((D, D), const),
            pl.BlockSpec((D, D), const),
            pl.BlockSpec((D, D), const),
            pl.BlockSpec((16, D), const),
            pl.BlockSpec((NH, 1), const),
            pl.BlockSpec((1, D), const),
        ],
        out_specs=_tok_specs(D) + _tok_specs(D) + [
            pl.BlockSpec((TM, D), ptile),
            pl.BlockSpec((D, TM), lambda i: (0, jnp.minimum(i, NPT - 1))),
            pl.BlockSpec((TM, DH), ptile),
            pl.BlockSpec((NH, TM), lambda i: (0, i)),
            pl.BlockSpec((NH, TM), lambda i: (0, i)),
        ],
        out_shape=_tok_shapes(D, F32) + _tok_shapes(D, F32) + [
            jax.ShapeDtypeStruct((NP, D), BF),
            jax.ShapeDtypeStruct((D, NP), BF),
            jax.ShapeDtypeStruct((NP, DH), BF),
            jax.ShapeDtypeStruct((NH, N), F32),
            jax.ShapeDtypeStruct((NH, N), F32),
        ],
        scratch_shapes=[pltpu.VMEM((NH, 128), F32)],
        compiler_params=_params("arbitrary"),
        name="shared_kv",
    )(h, g, wk, wv, wvt, wf, bf, kn)


def _qproj_kernel(x_ref, g_ref, wq_ref, qn_ref, q_ref):
    xn = _rms(x_ref[...], g_ref[...]).astype(BF)
    q_ref[...] = _head_rms(_dot(xn, wq_ref[...]), qn_ref[...]) * QSCALE


def _qproj(h, g, wq, qn):
    const = lambda i: (0, 0)
    tok = pl.BlockSpec((TM, D), lambda i: (i, 0))
    return pl.pallas_call(
        _qproj_kernel,
        grid=(N_TILES,),
        in_specs=[tok, pl.BlockSpec((1, D), const), pl.BlockSpec((D, D), const), pl.BlockSpec((1, D), const)],
        out_specs=tok,
        out_shape=jax.ShapeDtypeStruct((N, D), F32),
        compiler_params=_params("parallel"),
        name="qproj",
    )(h, g, wq, qn)


def _oproj_kernel(x_ref, op_ref, os_ref, wo_ref, y_ref):
    o = jnp.where(pl.program_id(0) == NPT, os_ref[...].astype(BF), op_ref[...])
    y_ref[...] = x_ref[...] + _dot(o, wo_ref[...])


def _oproj(h, o_p, o_s, wo):
    tok = pl.BlockSpec((TM, D), lambda i: (i, 0))
    return pl.pallas_call(
        _oproj_kernel,
        grid=(N_TILES,),
        in_specs=[tok] + _tok_specs(D) + [pl.BlockSpec((D, D), lambda i: (0, 0))],
        out_specs=tok,
        out_shape=jax.ShapeDtypeStruct((N, D), F32),
        compiler_params=_params("arbitrary"),
        name="oproj",
    )(h, o_p, o_s, wo)


def _attn_prompt_kernel(q_ref, k_ref, a_ref, vt_ref, c_ref, o_ref):
    h = pl.program_id(1)
    row = lax.broadcasted_iota(jnp.int32, (DH, TQ), 0)
    own = jnp.logical_and(row < 3 * NH, (row & (NH - 1)) == h)
    key = lax.broadcasted_iota(jnp.int32, (TQ, TQ), 0)
    qry = lax.broadcasted_iota(jnp.int32, (TQ, TQ), 1)

    def query_block(qi):
        q0 = qi * TQ
        hi, mid, lo = _split3(c_ref[0, :, q0:q0 + TQ] * LOG2E)
        aug = jnp.where(row == 3 * NH, hi, jnp.where(row == 3 * NH + 1, mid, jnp.where(row == 3 * NH + 2, lo, 0.0)))
        aug = jnp.where(own, 1.0, aug)
        return jnp.concatenate([jnp.transpose(q_ref[q0:q0 + TQ, :]), aug], axis=0).astype(BF)

    def scores(qa, qi, g):
        k0 = g * TQ
        kaug = jnp.concatenate([k_ref[k0:k0 + TQ, :], a_ref[k0:k0 + TQ, :]], axis=1)
        s = _dot(kaug, qa)
        return jnp.where(key <= qry, s, -jnp.inf) if g == qi else s

    pairs = [(qi, g) for qi in range(SEQ // TQ) for g in range(qi + 1)]
    qa = query_block(0)
    s_next = scores(qa, 0, 0)
    m = l = acc = pending = None
    for n, (qi, g) in enumerate(pairs):
        s = s_next
        if n + 1 < len(pairs):
            qi1, g1 = pairs[n + 1]
            if g1 == 0:
                qa = query_block(qi1)
            s_next = scores(qa, qi1, g1)
        m_cur = jnp.max(s, axis=0, keepdims=True)
        if g == 0:
            m, alpha = m_cur, None
            p = jnp.exp2(s - m)
            l = jnp.sum(p, axis=0, keepdims=True)
        else:
            m_new = jnp.maximum(m, m_cur)
            alpha = jnp.exp2(m - m_new)
            p = jnp.exp2(s - m_new)
            l = alpha * l + jnp.sum(p, axis=0, keepdims=True)
            m = m_new
        if pending is not None:
            acc = pending()
        k0 = g * TQ

        def pending(acc=acc, alpha=alpha, p=p, k0=k0, qi=qi, g=g, l=l):
            pv = _dot(vt_ref[:, k0:k0 + TQ], p.astype(BF))
            new = pv if alpha is None else alpha * acc + pv
            if g == qi:
                o_ref[qi * TQ:(qi + 1) * TQ, :] = jnp.transpose(new / l).astype(BF)
            return new
    pending()


def _attn_prompt(q, kb, a, vt, cs3):
    return pl.pallas_call(
        _attn_prompt_kernel,
        grid=(BATCH, NH),
        in_specs=[
            pl.BlockSpec((SEQ, DH), lambda b, h: (b, h)),
            pl.BlockSpec((SEQ, DH), lambda b, h: (b, h)),
            pl.BlockSpec((SEQ, DH), lambda b, h: (b, 0)),
            pl.BlockSpec((DH, SEQ), lambda b, h: (h, b)),
            pl.BlockSpec((1, 1, SEQ), lambda b, h: (h, 0, b)),
        ],
        out_specs=pl.BlockSpec((SEQ, DH), lambda b, h: (b, h)),
        out_shape=jax.ShapeDtypeStruct((NP, D), BF),
        compiler_params=_params("parallel", "parallel"),
        name="attn_prompt",
    )(q, kb, a, vt, cs3)


def _suffix_kernel(x_ref, o_ref):
    x = x_ref[...]
    width = PAGE * NH
    lane = lax.broadcasted_iota(jnp.int32, x.shape, 1)
    y = x
    t = x
    sh = NH
    while sh < width:
        y = y + jnp.where(lane < width - sh, pltpu.roll(y, width - sh, 1), 0.0)
        t = t + pltpu.roll(t, sh, 1)
        sh *= 2
    o_ref[:, :width] = y - x
    o_ref[:, width:] = t


def _suffix(lf_rows):
    rows, width = lf_rows.shape
    tr = 256
    spec = pl.BlockSpec((tr, width), lambda i: (i, 0))
    return pl.pallas_call(
        _suffix_kernel,
        grid=(rows // tr,),
        in_specs=[spec],
        out_specs=pl.BlockSpec((tr, 2 * width), lambda i: (i, 0)),
        out_shape=jax.ShapeDtypeStruct((rows, 2 * width), F32),
        compiler_params=_params("parallel"),
        name="cache_suffix",
    )(lf_rows)


def _attn_sample_kernel(pt_ref, q_ref, kn_ref, vn_ref, cq_ref, ckn_ref, *rest):
    pp = PAGES_PER_STEP
    k_refs, v_refs, f_refs = rest[:pp], rest[pp:2 * pp], rest[2 * pp:3 * pp]
    o_ref, m_ref, l_ref, acc_ref, carry_ref = rest[3 * pp:]
    p = pl.program_id(1)
    q = q_ref[0]
    cq = cq_ref[0][:, 0:1] * LOG2E

    @pl.when(p == 0)
    def _():
        s = _dot_nt(q, kn_ref[0])
        s = s + cq - ckn_ref[0] * LOG2E
        row = lax.broadcasted_iota(jnp.int32, s.shape, 0)
        col = lax.broadcasted_iota(jnp.int32, s.shape, 1)
        ok = jnp.logical_and((col & (NH - 1)) == (row & (NH - 1)), (col >> 3) <= (row >> 3))
        s = jnp.where(ok, s, -jnp.inf)
        m = jnp.max(s, axis=1, keepdims=True)
        e = jnp.exp2(s - m)
        m_ref[...] = m
        l_ref[...] = jnp.sum(e, axis=1, keepdims=True)
        acc_ref[...] = _dot(e.astype(BF), vn_ref[0])
        carry_ref[...] = jnp.zeros_like(carry_ref)

    carry = carry_ref[...]
    tails = []
    for j in range(pp):
        tails.append(f_refs[j][0, :, :PAGE * NH] + carry)
        carry = carry + f_refs[j][0, :, PAGE * NH:]
    carry_ref[...] = carry
    tail = jnp.concatenate(tails, axis=1) * LOG2E
    k = jnp.concatenate([r[0].reshape(PAGE * NH, DH).astype(BF) for r in k_refs], axis=0)
    v = jnp.concatenate([r[0].reshape(PAGE * NH, DH).astype(BF) for r in v_refs], axis=0)
    s = _dot_nt(q, k)
    row = lax.broadcasted_iota(jnp.int32, s.shape, 0)
    col = lax.broadcasted_iota(jnp.int32, s.shape, 1)
    s = jnp.where((col & (NH - 1)) == (row & (NH - 1)), s + cq + tail, -jnp.inf)
    m_prev = m_ref[...]
    m_new = jnp.maximum(m_prev, jnp.max(s, axis=1, keepdims=True))
    alpha = jnp.exp2(m_prev - m_new)
    e = jnp.exp2(s - m_new)
    l_ref[...] = alpha * l_ref[...] + jnp.sum(e, axis=1, keepdims=True)
    acc_ref[...] = alpha * acc_ref[...] + _dot(e.astype(BF), v)
    m_ref[...] = m_new

    @pl.when(p == pl.num_programs(1) - 1)
    def _():
        o_ref[0] = acc_ref[...] / l_ref[...]


def _attn_sample(page_table, q, kn, vn, cq, ckn, cache_k, cache_v, forget):
    pp = PAGES_PER_STEP

    def seq(b, p, pt):
        return (b, 0, 0)

    def page3(j):
        return lambda b, p, pt: (pt[b, N_PAGES - 1 - (pp * p + j)], 0, 0)

    def page4(j):
        return lambda b, p, pt: (pt[b, N_PAGES - 1 - (pp * p + j)], 0, 0, 0)

    in_specs = [
        pl.BlockSpec((1, ROWS, DH), seq),
        pl.BlockSpec((1, 128, DH), seq),
        pl.BlockSpec((1, 128, DH), seq),
        pl.BlockSpec((1, ROWS, 128), seq),
        pl.BlockSpec((1, 1, 128), seq),
    ]
    in_specs += [pl.BlockSpec((1, PAGE, NH, DH), page4(j)) for j in range(pp)]
    in_specs += [pl.BlockSpec((1, PAGE, NH, DH), page4(j)) for j in range(pp)]
    in_specs += [pl.BlockSpec((1, 1, 2 * PAGE * NH), page3(j)) for j in range(pp)]
    grid_spec = pltpu.PrefetchScalarGridSpec(
        num_scalar_prefetch=1,
        grid=(DEC_B, N_PAGES // pp),
        in_specs=in_specs,
        out_specs=pl.BlockSpec((1, ROWS, DH), seq),
        scratch_shapes=[
            pltpu.VMEM((ROWS, 1), F32),
            pltpu.VMEM((ROWS, 1), F32),
            pltpu.VMEM((ROWS, DH), F32),
            pltpu.VMEM((1, PAGE * NH), F32),
        ],
    )
    return pl.pallas_call(
        _attn_sample_kernel,
        grid_spec=grid_spec,
        out_shape=jax.ShapeDtypeStruct((DEC_B, ROWS, DH), F32),
        compiler_params=_params("parallel", "arbitrary"),
        name="attn_sample",
    )(page_table, q, kn, vn, cq, ckn,
      *([cache_k] * pp), *([cache_v] * pp), *([forget] * pp))


def kernel(x_prompt, x_sample, p_prompt, p_sample, cache_k, cache_v, cache_logf, page_table,
           ffn1_norm, ffn1_w1, ffn1_w3, ffn1_w2, mix_norm, ffn2_norm, ffn2_w1, ffn2_w3, ffn2_w2,
           ple_norm, ple_w_gate, ple_w_proj, gmlp_w_in, gmlp_v_norm, gmlp_w_s, gmlp_b_s, gmlp_w_out,
           kv_norm, w_kvf, b_f, k_norm, att_w_q, q_norm, att_w_o):
    bf = lambda w: w.astype(BF)
    row = lambda g: g.reshape(1, -1)

    h = jnp.concatenate([x_prompt.reshape(NP, D), x_sample.reshape(NS, D)], axis=0)
    pe_p, pe_s = p_prompt.reshape(DEPTH, NP, PLE), p_sample.reshape(DEPTH, NS, PLE)

    tril = jnp.tril(jnp.ones((DH, DH), F32))
    w_prompt = gmlp_w_s * tril
    small = (gmlp_w_s[:, :, :DEC_T, :DEC_T] * tril[:DEC_T, :DEC_T])
    eye = jnp.eye(DH // DEC_T, dtype=F32)
    w_sample = jnp.einsum('ab,lgts->lgatbs', eye, small).reshape(N_A, NH, DH, DH)
    wmix = bf(jnp.stack([w_prompt, w_sample], axis=1))
    b_prompt = jnp.repeat(gmlp_b_s.transpose(0, 2, 1), DH, axis=2)
    b_sample = jnp.tile(b_prompt[:, :DEC_T], (1, DH // DEC_T, 1))
    bmix = jnp.stack([b_prompt, b_sample], axis=1)

    wk = bf(w_kvf[:, :D])
    wv = bf(w_kvf[:, D:2 * D])
    wf = bf(jnp.pad(w_kvf[:, 2 * D:].T, ((0, 16 - NH), (0, 0))))

    kv = None
    gmlp_vs = []
    for i in range(DEPTH):
        h = _ffn(h, row(ffn1_norm[i]), bf(ffn1_w1[i]), bf(ffn1_w3[i]), bf(ffn1_w2[i]))
        if i < N_A:
            h, v_rows = _gmlp(h, row(mix_norm[i]), bf(gmlp_w_in[i]), row(gmlp_v_norm[i]),
                              wmix[i], bmix[i], bf(gmlp_w_out[i]))
            gmlp_vs.append(v_rows.reshape(DEC_B, DEC_T, D))
        else:
            j = i - N_A
            k32s, v32s, kb, vt, a, cst, forget = kv
            q = _qproj(h, row(mix_norm[i]), bf(att_w_q[j]), row(jnp.tile(q_norm[j], NH)))
            o_p = _attn_prompt(q, kb, a, vt, cst.reshape(NH, 1, N))
            heads = lambda x: bf(x).reshape(DEC_B, ROWS, DH)
            pad_rows = lambda x: jnp.pad(x, ((0, 0), (0, 128 - ROWS), (0, 0)))
            c_new = cst[:, NP:].reshape(NH, DEC_B, DEC_T).transpose(1, 2, 0).reshape(DEC_B, ROWS)
            cq = jnp.broadcast_to(c_new[:, :, None], (DEC_B, ROWS, 128))
            ckn = jnp.pad(c_new, ((0, 0), (0, 128 - ROWS)))[:, None, :]
            o_s = _attn_sample(page_table, heads(q[NP:]), pad_rows(heads(k32s)), pad_rows(heads(v32s)),
                               cq, ckn, cache_k, cache_v, forget)
            h = _oproj(h, o_p, o_s.reshape(NS, D), bf(att_w_o[j]))
        h = _ffn(h, row(ffn2_norm[i]), bf(ffn2_w1[i]), bf(ffn2_w3[i]), bf(ffn2_w2[i]))
        h = _ple(h, row(ple_norm[i]), bf(ple_w_gate[i]), pe_p, pe_s, i, bf(ple_w_proj[i]),
                 split_out=i == DEPTH - 1)
        if i == N_A - 1:
            k32p, k32s, v32p, v32s, kb, vt, a, lft, cst = _kv(
                h, row(kv_norm), wk, wv, bf(w_kvf[:, D:2 * D].T), wf,
                b_f.reshape(NH, 1), row(jnp.tile(k_norm, NH)))
            n_pool = cache_logf.shape[0]
            forget = _suffix(cache_logf.reshape(n_pool, PAGE * NH)).reshape(n_pool, 1, 2 * PAGE * NH)
            kv = (k32s, v32s, kb, vt, a, cst, forget)

    hp, hs = h
    lf_tok = lft.T
    return (
        hp.reshape(BATCH, SEQ, D),
        hs.reshape(DEC_B, DEC_T, D),
        k32p.reshape(BATCH, SEQ, NH, DH),
        v32p.reshape(BATCH, SEQ, NH, DH),
        lf_tok[:NP].reshape(BATCH, SEQ, NH),
        k32s.reshape(DEC_B, DEC_T, NH, DH),
        v32s.reshape(DEC_B, DEC_T, NH, DH),
        lf_tok[NP:].reshape(DEC_B, DEC_T, NH),
        jnp.stack(gmlp_vs),
    )
```

```python
import jax
import jax.numpy as jnp
from jax import lax
from jax.experimental import pallas as pl
from jax.experimental.pallas import tpu as pltpu

D = 1024
DFF = 2816
NH = 8
DH = 128
PLE = 256
DEPTH = 4
N_A = 2
BATCH, SEQ = 4, 4096
DEC_B, DEC_T = 128, 4
PAGE = 128
N_PAGES = 16
NP = BATCH * SEQ
NS = DEC_B * DEC_T
N = NP + NS
EPS = 1e-6
LOG2E = 1.4426950408889634
QSCALE = DH ** -0.5 * LOG2E

BF = jnp.bfloat16
F32 = jnp.float32

TM = 512
NPT = NP // TM
N_TILES = NPT + 1
FFN_TM = 768
FFN_TF = 768
FFN_EDGE = 256
TQ = 512
SCORE_LOOKAHEAD = 2
PAGES_PER_STEP = 8
PAGE_SLOTS = 3
ROWS = DEC_T * NH
VMEM_LIMIT = 56 * 1024 * 1024

NT_DIMS = (((1,), (1,)), ((), ()))


def _params(*sem):
    return pltpu.CompilerParams(dimension_semantics=sem, vmem_limit_bytes=VMEM_LIMIT)


def _resident(shape):
    return pl.BlockSpec(shape, lambda *_: (0,) * len(shape), pipeline_mode=pl.Buffered(1))


def _tok_specs(width):
    return [pl.BlockSpec((TM, width), lambda i: (jnp.minimum(i, NPT - 1), 0)),
            pl.BlockSpec((NS, width), lambda i: (0, 0))]


def _tok_shapes(width, dtype):
    return [jax.ShapeDtypeStruct((NP, width), dtype), jax.ShapeDtypeStruct((NS, width), dtype)]


def _load_tok(p_ref, s_ref):
    return jnp.where(pl.program_id(0) == NPT, s_ref[...], p_ref[...])


def _store_tok(p_ref, s_ref, y):
    i = pl.program_id(0)

    @pl.when(i < NPT)
    def _():
        p_ref[...] = y

    @pl.when(i == NPT)
    def _():
        s_ref[...] = y


def _rms(x, g):
    ms = jnp.mean(x * x, axis=-1, keepdims=True)
    return x * lax.rsqrt(ms + EPS) * g


def _head_rms(x, g):
    outs = []
    for h in range(NH):
        xh = x[:, h * DH:(h + 1) * DH]
        ms = jnp.mean(xh * xh, axis=-1, keepdims=True)
        outs.append(xh * lax.rsqrt(ms + EPS))
    return jnp.concatenate(outs, axis=-1) * g


def _dot(a, b):
    return jnp.dot(a, b, preferred_element_type=F32)


def _dot_nt(a, b):
    return lax.dot_general(a, b, NT_DIMS, preferred_element_type=F32)


def _split3(x):
    hi = x.astype(BF).astype(F32)
    r = x - hi
    mid = r.astype(BF).astype(F32)
    return hi, mid, r - mid


def _ffn_kernel(x_ref, g_ref, w1_ref, w3_ref, w2_ref, o_ref, xn_ref, acc_ref):
    def chunk(f0, width):
        xn = xn_ref[...]
        h1 = _dot(xn, w1_ref[:, pl.ds(f0, width)])
        h3 = _dot(xn, w3_ref[:, pl.ds(f0, width)])
        a = (h1 * jax.nn.sigmoid(h1) * h3).astype(BF)
        return _dot(a, w2_ref[pl.ds(f0, width), :])

    xn_ref[...] = _rms(x_ref[...], g_ref[...]).astype(BF)
    acc_ref[...] = chunk(0, FFN_EDGE)

    for f0 in range(FFN_EDGE, DFF - FFN_EDGE, FFN_TF):
        acc_ref[...] += chunk(f0, FFN_TF)
    o_ref[...] = x_ref[...] + 0.5 * (acc_ref[...] + chunk(DFF - FFN_EDGE, FFN_EDGE))


def _ffn(h, g, w1, w3, w2):
    tm = FFN_TM
    return pl.pallas_call(
        _ffn_kernel,
        grid=(N // tm,),
        in_specs=[pl.BlockSpec((tm, D), lambda i: (i, 0)),
                  _resident((1, D)), _resident((D, DFF)), _resident((D, DFF)), _resident((DFF, D))],
        out_specs=pl.BlockSpec((tm, D), lambda i: (i, 0)),
        out_shape=jax.ShapeDtypeStruct((N, D), F32),
        scratch_shapes=[pltpu.VMEM((tm, D), BF), pltpu.VMEM((tm, D), F32)],
        compiler_params=_params("parallel"),
        name="ffn",
    )(h, g, w1, w3, w2)


def _gmlp_kernel(x_ref, g_ref, win_ref, vn_ref, wmix_ref, bmix_ref, wout_ref, o_ref, vs_ref):
    x = x_ref[...]
    xn = _rms(x, g_ref[...]).astype(BF)
    uv = _dot(xn, win_ref[...])
    u = uv[:, :D]
    v = _rms(uv[:, D:], vn_ref[...])

    @pl.when(pl.program_id(0) == NPT)
    def _():
        vs_ref[...] = v

    vb = v.astype(BF)
    bias = bmix_ref[...]
    rows = []
    for c in range(TM // DH):
        cols = []
        for g in range(NH):
            blk = vb[c * DH:(c + 1) * DH, g * DH:(g + 1) * DH]
            cols.append(_dot(wmix_ref[g], blk))
        rows.append(jnp.concatenate(cols, axis=1) + bias)
    s = jnp.concatenate(rows, axis=0)
    m = (u * s).astype(BF)
    o_ref[...] = x + _dot(m, wout_ref[...])


def _gmlp(h, g, w_in, v_norm, wmix, bmix, w_out):
    const = lambda i: (0, 0)
    tok = pl.BlockSpec((TM, D), lambda i: (i, 0))
    return pl.pallas_call(
        _gmlp_kernel,
        grid=(N_TILES,),
        in_specs=[
            tok,
            pl.BlockSpec((1, D), const),
            pl.BlockSpec((D, 2 * D), const),
            pl.BlockSpec((1, D), const),
            pl.BlockSpec((None, NH, DH, DH), lambda i: (i // NPT, 0, 0, 0)),
            pl.BlockSpec((None, DH, D), lambda i: (i // NPT, 0, 0)),
            pl.BlockSpec((D, D), const),
        ],
        out_specs=[tok, pl.BlockSpec((NS, D), const)],
        out_shape=[jax.ShapeDtypeStruct((N, D), F32), jax.ShapeDtypeStruct((NS, D), F32)],
        compiler_params=_params("arbitrary"),
        name="gmlp",
    )(h, g, w_in, v_norm, wmix, bmix, w_out)


def _ple_kernel(x_ref, g_ref, wg_ref, pep_ref, pes_ref, wp_ref, *o_refs):
    x = x_ref[...]
    xn = _rms(x, g_ref[...]).astype(BF)
    gate = jax.nn.sigmoid(_dot(xn, wg_ref[...]))
    proj = _dot(_load_tok(pep_ref, pes_ref).astype(BF), wp_ref[...])
    y = x + gate * proj
    if len(o_refs) == 1:
        o_refs[0][...] = y
    else:
        _store_tok(*o_refs, y)


def _ple(h, g, w_gate, pe_p, pe_s, layer, w_proj, split_out):
    const = lambda i: (0, 0)
    tok = pl.BlockSpec((TM, D), lambda i: (i, 0))
    return pl.pallas_call(
        _ple_kernel,
        grid=(N_TILES,),
        in_specs=[
            tok,
            pl.BlockSpec((1, D), const),
            pl.BlockSpec((D, D), const),
            pl.BlockSpec((None, TM, PLE), lambda i: (layer, jnp.minimum(i, NPT - 1), 0)),
            pl.BlockSpec((None, NS, PLE), lambda i: (layer, 0, 0)),
            pl.BlockSpec((PLE, D), const),
        ],
        out_specs=_tok_specs(D) if split_out else tok,
        out_shape=_tok_shapes(D, F32) if split_out else jax.ShapeDtypeStruct((N, D), F32),
        compiler_params=_params("arbitrary"),
        name="ple",
    )(h, g, w_gate, pe_p, pe_s, w_proj)


def _kv_kernel(x_ref, g_ref, wk_ref, wv_ref, wvt_ref, wf_ref, bf_ref, kn_ref,
               k32p_ref, k32s_ref, v32p_ref, v32s_ref, kb_ref, vt_ref, a_ref, lf_ref, cs_ref, carry_ref):
    i = pl.program_id(0)
    is_sample = i == NPT
    xs = _rms(x_ref[...], g_ref[...]).astype(BF)
    k = _head_rms(_dot(xs, wk_ref[...]), kn_ref[...])
    _store_tok(k32p_ref, k32s_ref, k)
    _store_tok(v32p_ref, v32s_ref, _dot(xs, wv_ref[...]))

    f = _dot_nt(wf_ref[...], xs)[:NH] + bf_ref[...]
    lf = jnp.minimum(f, 0.0) - jnp.log1p(jnp.exp(-jnp.abs(f)))
    lf_ref[...] = lf

    lane = lax.broadcasted_iota(jnp.int32, (NH, TM), 1)
    pos = lane & jnp.where(is_sample, DEC_T - 1, TM - 1)
    y = lf
    sh = 1
    while sh < TM:
        y = y + jnp.where(pos >= sh, pltpu.roll(y, sh, 1), 0.0)
        sh *= 2

    @pl.when(jnp.logical_or(i % (SEQ // TM) == 0, is_sample))
    def _():
        carry_ref[...] = jnp.zeros_like(carry_ref)

    y = y + carry_ref[:, 0:1]
    cs_ref[...] = y
    carry_ref[...] = jnp.broadcast_to(y[:, TM - 1:TM], carry_ref.shape)

    @pl.when(i < NPT)
    def _():
        kb_ref[...] = k.astype(BF)
        vt_ref[...] = _dot_nt(wvt_ref[...], xs).astype(BF)
        hi, mid, lo = _split3(-(y * LOG2E))
        row = lax.broadcasted_iota(jnp.int32, (NH, TM), 0)
        ones = jnp.where(row < 3, 1.0, 0.0)
        pad = jnp.zeros((DH - 4 * NH, TM), F32)
        a_ref[...] = jnp.transpose(jnp.concatenate([hi, mid, lo, ones, pad], axis=0)).astype(BF)


def _kv(h, g, wk, wv, wvt, wf, bf, kn):
    const = lambda i: (0, 0)
    ptile = lambda i: (jnp.minimum(i, NPT - 1), 0)
    return pl.pallas_call(
        _kv_kernel,
        grid=(N_TILES,),
        in_specs=[
            pl.BlockSpec((TM, D), lambda i: (i, 0)),
            pl.BlockSpec((1, D), const),
            pl.BlockSpec((D, D), const),
            pl.BlockSpec((D, D), const),
            pl.BlockSpec((D, D), const),
            pl.BlockSpec((16, D), const),
            pl.BlockSpec((NH, 1), const),
            pl.BlockSpec((1, D), const),
        ],
        out_specs=_tok_specs(D) + _tok_specs(D) + [
            pl.BlockSpec((TM, D), ptile),
            pl.BlockSpec((D, TM), lambda i: (0, jnp.minimum(i, NPT - 1))),
            pl.BlockSpec((TM, DH), ptile),
            pl.BlockSpec((NH, TM), lambda i: (0, i)),
            pl.BlockSpec((NH, TM), lambda i: (0, i)),
        ],
        out_shape=_tok_shapes(D, F32) + _tok_shapes(D, F32) + [
            jax.ShapeDtypeStruct((NP, D), BF),
            jax.ShapeDtypeStruct((D, NP), BF),
            jax.ShapeDtypeStruct((NP, DH), BF),
            jax.ShapeDtypeStruct((NH, N), F32),
            jax.ShapeDtypeStruct((NH, N), F32),
        ],
        scratch_shapes=[pltpu.VMEM((NH, 128), F32)],
        compiler_params=_params("arbitrary"),
        name="shared_kv",
    )(h, g, wk, wv, wvt, wf, bf, kn)


def _qproj_kernel(x_ref, g_ref, wq_ref, qn_ref, q_ref):
    xn = _rms(x_ref[...], g_ref[...]).astype(BF)
    q_ref[...] = _head_rms(_dot(xn, wq_ref[...]), qn_ref[...]) * QSCALE


def _qproj(h, g, wq, qn):
    const = lambda i: (0, 0)
    tok = pl.BlockSpec((TM, D), lambda i: (i, 0))
    return pl.pallas_call(
        _qproj_kernel,
        grid=(N_TILES,),
        in_specs=[tok, pl.BlockSpec((1, D), const), pl.BlockSpec((D, D), const), pl.BlockSpec((1, D), const)],
        out_specs=tok,
        out_shape=jax.ShapeDtypeStruct((N, D), F32),
        compiler_params=_params("parallel"),
        name="qproj",
    )(h, g, wq, qn)


def _oproj_kernel(x_ref, op_ref, os_ref, wo_ref, y_ref):
    o = jnp.where(pl.program_id(0) == NPT, os_ref[...].astype(BF), op_ref[...])
    y_ref[...] = x_ref[...] + _dot(o, wo_ref[...])


def _oproj(h, o_p, o_s, wo):
    tok = pl.BlockSpec((TM, D), lambda i: (i, 0))
    return pl.pallas_call(
        _oproj_kernel,
        grid=(N_TILES,),
        in_specs=[tok] + _tok_specs(D) + [pl.BlockSpec((D, D), lambda i: (0, 0))],
        out_specs=tok,
        out_shape=jax.ShapeDtypeStruct((N, D), F32),
        compiler_params=_params("arbitrary"),
        name="oproj",
    )(h, o_p, o_s, wo)


def _attn_prompt_kernel(q_ref, k_ref, a_ref, vt_ref, c_ref, o_ref):
    h = pl.program_id(1)
    row = lax.broadcasted_iota(jnp.int32, (DH, TQ), 0)
    own = jnp.logical_and(row < 3 * NH, (row & (NH - 1)) == h)
    key = lax.broadcasted_iota(jnp.int32, (TQ, TQ), 0)
    qry = lax.broadcasted_iota(jnp.int32, (TQ, TQ), 1)

    def query_block(qi):
        q0 = qi * TQ
        hi, mid, lo = _split3(c_ref[0, :, q0:q0 + TQ] * LOG2E)
        aug = jnp.where(row == 3 * NH, hi, jnp.where(row == 3 * NH + 1, mid, jnp.where(row == 3 * NH + 2, lo, 0.0)))
        aug = jnp.where(own, 1.0, aug)
        return jnp.concatenate([jnp.transpose(q_ref[q0:q0 + TQ, :]), aug], axis=0).astype(BF)

    def scores(qa, qi, g):
        k0 = g * TQ
        kaug = jnp.concatenate([k_ref[k0:k0 + TQ, :], a_ref[k0:k0 + TQ, :]], axis=1)
        s = _dot(kaug, qa)
        return jnp.where(key <= qry, s, -jnp.inf) if g == qi else s

    pairs = [(qi, g) for qi in range(SEQ // TQ) for g in range(qi + 1)]
    qa = None
    queue = []

    def issue_scores(n):
        nonlocal qa
        if n < len(pairs):
            qi1, g1 = pairs[n]
            if g1 == 0:
                qa = query_block(qi1)
            queue.append(scores(qa, qi1, g1))

    for n in range(SCORE_LOOKAHEAD):
        issue_scores(n)
    m = l = acc = pending = None
    for n, (qi, g) in enumerate(pairs):
        issue_scores(n + SCORE_LOOKAHEAD)
        s = queue.pop(0)
        m_cur = jnp.max(s, axis=0, keepdims=True)
        if g == 0:
            m, alpha = m_cur, None
            p = jnp.exp2(s - m)
            l = jnp.sum(p, axis=0, keepdims=True)
        else:
            m_new = jnp.maximum(m, m_cur)
            alpha = jnp.exp2(m - m_new)
            p = jnp.exp2(s - m_new)
            l = alpha * l + jnp.sum(p, axis=0, keepdims=True)
            m = m_new
        if pending is not None:
            acc = pending()
        k0 = g * TQ

        def pending(acc=acc, alpha=alpha, p=p, k0=k0, qi=qi, g=g, l=l):
            pv = _dot(vt_ref[:, k0:k0 + TQ], p.astype(BF))
            new = pv if alpha is None else alpha * acc + pv
            if g == qi:
                o_ref[qi * TQ:(qi + 1) * TQ, :] = jnp.transpose(new / l).astype(BF)
            return new
    pending()


def _attn_prompt(q, kb, a, vt, cs3):
    return pl.pallas_call(
        _attn_prompt_kernel,
        grid=(BATCH, NH),
        in_specs=[
            pl.BlockSpec((SEQ, DH), lambda b, h: (b, h)),
            pl.BlockSpec((SEQ, DH), lambda b, h: (b, h)),
            pl.BlockSpec((SEQ, DH), lambda b, h: (b, 0)),
            pl.BlockSpec((DH, SEQ), lambda b, h: (h, b)),
            pl.BlockSpec((1, 1, SEQ), lambda b, h: (h, 0, b)),
        ],
        out_specs=pl.BlockSpec((SEQ, DH), lambda b, h: (b, h)),
        out_shape=jax.ShapeDtypeStruct((NP, D), BF),
        compiler_params=_params("parallel", "parallel"),
        name="attn_prompt",
    )(q, kb, a, vt, cs3)


def _suffix_kernel(x_ref, o_ref):
    x = x_ref[...]
    width = PAGE * NH
    lane = lax.broadcasted_iota(jnp.int32, x.shape, 1)
    y = x
    t = x
    sh = NH
    while sh < width:
        y = y + jnp.where(lane < width - sh, pltpu.roll(y, width - sh, 1), 0.0)
        t = t + pltpu.roll(t, sh, 1)
        sh *= 2
    o_ref[:, :width] = y - x
    o_ref[:, width:] = t


def _suffix(lf_rows):
    rows, width = lf_rows.shape
    tr = 256
    spec = pl.BlockSpec((tr, width), lambda i: (i, 0))
    return pl.pallas_call(
        _suffix_kernel,
        grid=(rows // tr,),
        in_specs=[spec],
        out_specs=pl.BlockSpec((tr, 2 * width), lambda i: (i, 0)),
        out_shape=jax.ShapeDtypeStruct((rows, 2 * width), F32),
        compiler_params=_params("parallel"),
        name="cache_suffix",
    )(lf_rows)


def _attn_sample_kernel(pt_ref, q_ref, kn_ref, vn_ref, cq_ref, ckn_ref, ck_hbm, cv_hbm, f_hbm, o_ref,
                        kbuf, vbuf, fbuf, sems, m_ref, l_ref, acc_ref, carry_ref):
    pp = PAGES_PER_STEP
    steps_per_seq = N_PAGES // pp
    n_steps = DEC_B * steps_per_seq
    b = pl.program_id(0)
    p = pl.program_id(1)
    t = b * steps_per_seq + p

    def page_copies(step, slot):
        sb = step // steps_per_seq
        sp = step % steps_per_seq
        copies = []
        for j in range(pp):
            page = pt_ref[sb, N_PAGES - 1 - (pp * sp + j)]
            copies.append(pltpu.make_async_copy(ck_hbm.at[page], kbuf.at[slot, j], sems.at[0, slot]))
            copies.append(pltpu.make_async_copy(cv_hbm.at[page], vbuf.at[slot, j], sems.at[1, slot]))
            copies.append(pltpu.make_async_copy(f_hbm.at[page], fbuf.at[slot, j], sems.at[2, slot]))
        return copies

    def fetch(step, slot):
        for c in page_copies(step, slot):
            c.start()

    @pl.when(t == 0)
    def _():
        fetch(0, 0)
        fetch(1, 1)

    @pl.when(t + 2 < n_steps)
    def _():
        fetch(t + 2, (t + 2) % PAGE_SLOTS)

    slot = t % PAGE_SLOTS
    for c in page_copies(t, slot):
        c.wait()

    q = q_ref[0]
    cq = cq_ref[0][:, 0:1] * LOG2E

    @pl.when(p == 0)
    def _():
        s = _dot_nt(q, kn_ref[0])
        s = s + cq - ckn_ref[0] * LOG2E
        row = lax.broadcasted_iota(jnp.int32, s.shape, 0)
        col = lax.broadcasted_iota(jnp.int32, s.shape, 1)
        ok = jnp.logical_and((col & (NH - 1)) == (row & (NH - 1)), (col >> 3) <= (row >> 3))
        s = jnp.where(ok, s, -jnp.inf)
        m = jnp.max(s, axis=1, keepdims=True)
        e = jnp.exp2(s - m)
        m_ref[...] = m
        l_ref[...] = jnp.sum(e, axis=1, keepdims=True)
        acc_ref[...] = _dot(e.astype(BF), vn_ref[0])
        carry_ref[...] = jnp.zeros_like(carry_ref)

    carry = carry_ref[...]
    tails = []
    for j in range(pp):
        tails.append(fbuf[slot, j, :, :PAGE * NH] + carry)
        carry = carry + fbuf[slot, j, :, PAGE * NH:]
    carry_ref[...] = carry
    tail = jnp.concatenate(tails, axis=1) * LOG2E
    k = jnp.concatenate([kbuf[slot, j].reshape(PAGE * NH, DH).astype(BF) for j in range(pp)], axis=0)
    v = jnp.concatenate([vbuf[slot, j].reshape(PAGE * NH, DH).astype(BF) for j in range(pp)], axis=0)
    s = _dot_nt(q, k)
    row = lax.broadcasted_iota(jnp.int32, s.shape, 0)
    col = lax.broadcasted_iota(jnp.int32, s.shape, 1)
    s = jnp.where((col & (NH - 1)) == (row & (NH - 1)), s + cq + tail, -jnp.inf)
    m_prev = m_ref[...]
    m_new = jnp.maximum(m_prev, jnp.max(s, axis=1, keepdims=True))
    alpha = jnp.exp2(m_prev - m_new)
    e = jnp.exp2(s - m_new)
    l_ref[...] = alpha * l_ref[...] + jnp.sum(e, axis=1, keepdims=True)
    acc_ref[...] = alpha * acc_ref[...] + _dot(e.astype(BF), v)
    m_ref[...] = m_new

    @pl.when(p == steps_per_seq - 1)
    def _():
        o_ref[0] = acc_ref[...] / l_ref[...]


def _attn_sample(page_table, q, kn, vn, cq, ckn, cache_k, cache_v, forget):
    pp = PAGES_PER_STEP

    def seq(b, p, pt):
        return (b, 0, 0)

    hbm = pl.BlockSpec(memory_space=pl.ANY)
    grid_spec = pltpu.PrefetchScalarGridSpec(
        num_scalar_prefetch=1,
        grid=(DEC_B, N_PAGES // pp),
        in_specs=[
            pl.BlockSpec((1, ROWS, DH), seq),
            pl.BlockSpec((1, 128, DH), seq),
            pl.BlockSpec((1, 128, DH), seq),
            pl.BlockSpec((1, ROWS, 128), seq),
            pl.BlockSpec((1, 1, 128), seq),
            hbm, hbm, hbm,
        ],
        out_specs=pl.BlockSpec((1, ROWS, DH), seq),
        scratch_shapes=[
            pltpu.VMEM((PAGE_SLOTS, pp, PAGE, NH, DH), F32),
            pltpu.VMEM((PAGE_SLOTS, pp, PAGE, NH, DH), F32),
            pltpu.VMEM((PAGE_SLOTS, pp, 1, 2 * PAGE * NH), F32),
            pltpu.SemaphoreType.DMA((3, PAGE_SLOTS)),
            pltpu.VMEM((ROWS, 1), F32),
            pltpu.VMEM((ROWS, 1), F32),
            pltpu.VMEM((ROWS, DH), F32),
            pltpu.VMEM((1, PAGE * NH), F32),
        ],
    )
    return pl.pallas_call(
        _attn_sample_kernel,
        grid_spec=grid_spec,
        out_shape=jax.ShapeDtypeStruct((DEC_B, ROWS, DH), F32),
        compiler_params=_params("arbitrary", "arbitrary"),
        name="attn_sample",
    )(page_table, q, kn, vn, cq, ckn, cache_k, cache_v, forget)


def kernel(x_prompt, x_sample, p_prompt, p_sample, cache_k, cache_v, cache_logf, page_table,
           ffn1_norm, ffn1_w1, ffn1_w3, ffn1_w2, mix_norm, ffn2_norm, ffn2_w1, ffn2_w3, ffn2_w2,
           ple_norm, ple_w_gate, ple_w_proj, gmlp_w_in, gmlp_v_norm, gmlp_w_s, gmlp_b_s, gmlp_w_out,
           kv_norm, w_kvf, b_f, k_norm, att_w_q, q_norm, att_w_o):
    bf = lambda w: w.astype(BF)
    row = lambda g: g.reshape(1, -1)

    h = jnp.concatenate([x_prompt.reshape(NP, D), x_sample.reshape(NS, D)], axis=0)
    pe_p, pe_s = p_prompt.reshape(DEPTH, NP, PLE), p_sample.reshape(DEPTH, NS, PLE)

    tril = jnp.tril(jnp.ones((DH, DH), F32))
    w_prompt = gmlp_w_s * tril
    small = (gmlp_w_s[:, :, :DEC_T, :DEC_T] * tril[:DEC_T, :DEC_T])
    eye = jnp.eye(DH // DEC_T, dtype=F32)
    w_sample = jnp.einsum('ab,lgts->lgatbs', eye, small).reshape(N_A, NH, DH, DH)
    wmix = bf(jnp.stack([w_prompt, w_sample], axis=1))
    b_prompt = jnp.repeat(gmlp_b_s.transpose(0, 2, 1), DH, axis=2)
    b_sample = jnp.tile(b_prompt[:, :DEC_T], (1, DH // DEC_T, 1))
    bmix = jnp.stack([b_prompt, b_sample], axis=1)

    wk = bf(w_kvf[:, :D])
    wv = bf(w_kvf[:, D:2 * D])
    wf = bf(jnp.pad(w_kvf[:, 2 * D:].T, ((0, 16 - NH), (0, 0))))

    kv = None
    gmlp_vs = []
    for i in range(DEPTH):
        h = _ffn(h, row(ffn1_norm[i]), bf(ffn1_w1[i]), bf(ffn1_w3[i]), bf(ffn1_w2[i]))
        if i < N_A:
            h, v_rows = _gmlp(h, row(mix_norm[i]), bf(gmlp_w_in[i]), row(gmlp_v_norm[i]),
                              wmix[i], bmix[i], bf(gmlp_w_out[i]))
            gmlp_vs.append(v_rows.reshape(DEC_B, DEC_T, D))
        else:
            j = i - N_A
            k32s, v32s, kb, vt, a, cst, forget = kv
            q = _qproj(h, row(mix_norm[i]), bf(att_w_q[j]), row(jnp.tile(q_norm[j], NH)))
            o_p = _attn_prompt(q, kb, a, vt, cst.reshape(NH, 1, N))
            heads = lambda x: bf(x).reshape(DEC_B, ROWS, DH)
            pad_rows = lambda x: jnp.pad(x, ((0, 0), (0, 128 - ROWS), (0, 0)))
            c_new = cst[:, NP:].reshape(NH, DEC_B, DEC_T).transpose(1, 2, 0).reshape(DEC_B, ROWS)
            cq = jnp.broadcast_to(c_new[:, :, None], (DEC_B, ROWS, 128))
            ckn = jnp.pad(c_new, ((0, 0), (0, 128 - ROWS)))[:, None, :]
            o_s = _attn_sample(page_table, heads(q[NP:]), pad_rows(heads(k32s)), pad_rows(heads(v32s)),
                               cq, ckn, cache_k, cache_v, forget)
            h = _oproj(h, o_p, o_s.reshape(NS, D), bf(att_w_o[j]))
        h = _ffn(h, row(ffn2_norm[i]), bf(ffn2_w1[i]), bf(ffn2_w3[i]), bf(ffn2_w2[i]))
        h = _ple(h, row(ple_norm[i]), bf(ple_w_gate[i]), pe_p, pe_s, i, bf(ple_w_proj[i]),
                 split_out=i == DEPTH - 1)
        if i == N_A - 1:
            k32p, k32s, v32p, v32s, kb, vt, a, lft, cst = _kv(
                h, row(kv_norm), wk, wv, bf(w_kvf[:, D:2 * D].T), wf,
                b_f.reshape(NH, 1), row(jnp.tile(k_norm, NH)))
            n_pool = cache_logf.shape[0]
            forget = _suffix(cache_logf.reshape(n_pool, PAGE * NH)).reshape(n_pool, 1, 2 * PAGE * NH)
            kv = (k32s, v32s, kb, vt, a, cst, forget)

    hp, hs = h
    lf_tok = lft.T
    return (
        hp.reshape(BATCH, SEQ, D),
        hs.reshape(DEC_B, DEC_T, D),
        k32p.reshape(BATCH, SEQ, NH, DH),
        v32p.reshape(BATCH, SEQ, NH, DH),
        lf_tok[:NP].reshape(BATCH, SEQ, NH),
        k32s.reshape(DEC_B, DEC_T, NH, DH),
        v32s.reshape(DEC_B, DEC_T, NH, DH),
        lf_tok[NP:].reshape(DEC_B, DEC_T, NH),
        jnp.stack(gmlp_vs),
    )
```

```python
import functools

import jax
import jax.numpy as jnp
from jax import lax
from jax.experimental import pallas as pl
from jax.experimental.pallas import tpu as pltpu

D = 1024
DFF = 2816
NH = 8
DH = 128
PLE = 256
DEPTH = 4
N_A = 2
BATCH, SEQ = 4, 4096
DEC_B, DEC_T = 128, 4
PAGE = 128
N_PAGES = 16
NP = BATCH * SEQ
NS = DEC_B * DEC_T
N = NP + NS
EPS = 1e-6
LOG2E = 1.4426950408889634
QSCALE = DH ** -0.5 * LOG2E

BF = jnp.bfloat16
F32 = jnp.float32

TM = 512
NPT = NP // TM
N_TILES = NPT + 1
FFN_TF = 768
FFN_EDGE = 256
TQ = 512
SCORE_LOOKAHEAD = 2
PAGES_PER_STEP = 8
PAGE_SLOTS = 3
ROWS = DEC_T * NH
VMEM_LIMIT = 56 * 1024 * 1024

NT_DIMS = (((1,), (1,)), ((), ()))


def _params(*sem):
    return pltpu.CompilerParams(dimension_semantics=sem, vmem_limit_bytes=VMEM_LIMIT)


def _resident(shape):
    return pl.BlockSpec(shape, lambda *_: (0,) * len(shape), pipeline_mode=pl.Buffered(1))


def _tok_specs(width):
    return [pl.BlockSpec((TM, width), lambda i: (jnp.minimum(i, NPT - 1), 0)),
            pl.BlockSpec((NS, width), lambda i: (0, 0))]


def _tok_shapes(width, dtype):
    return [jax.ShapeDtypeStruct((NP, width), dtype), jax.ShapeDtypeStruct((NS, width), dtype)]


def _load_tok(p_ref, s_ref):
    return jnp.where(pl.program_id(0) == NPT, s_ref[...], p_ref[...])


def _store_tok(p_ref, s_ref, y):
    i = pl.program_id(0)

    @pl.when(i < NPT)
    def _():
        p_ref[...] = y

    @pl.when(i == NPT)
    def _():
        s_ref[...] = y


def _rms(x, g):
    ms = jnp.mean(x * x, axis=-1, keepdims=True)
    return x * lax.rsqrt(ms + EPS) * g


def _head_rms(x, g):
    outs = []
    for h in range(NH):
        xh = x[:, h * DH:(h + 1) * DH]
        ms = jnp.mean(xh * xh, axis=-1, keepdims=True)
        outs.append(xh * lax.rsqrt(ms + EPS))
    return jnp.concatenate(outs, axis=-1) * g


def _dot(a, b):
    return jnp.dot(a, b, preferred_element_type=F32)


def _dot_nt(a, b):
    return lax.dot_general(a, b, NT_DIMS, preferred_element_type=F32)


def _split3(x):
    hi = x.astype(BF).astype(F32)
    r = x - hi
    mid = r.astype(BF).astype(F32)
    return hi, mid, r - mid


def _gmlp_tile(x, g_ref, win_ref, vn_ref, wmix_ref, bmix_ref, wout_ref, vs_ref):
    xn = _rms(x, g_ref[...]).astype(BF)
    uv = _dot(xn, win_ref[...])
    u = uv[:, :D]
    v = _rms(uv[:, D:], vn_ref[...])

    @pl.when(pl.program_id(0) == NPT)
    def _():
        vs_ref[...] = v

    vb = v.astype(BF)
    bias = bmix_ref[...]
    rows = []
    for c in range(TM // DH):
        cols = []
        for g in range(NH):
            blk = vb[c * DH:(c + 1) * DH, g * DH:(g + 1) * DH]
            cols.append(_dot(wmix_ref[g], blk))
        rows.append(jnp.concatenate(cols, axis=1) + bias)
    s = jnp.concatenate(rows, axis=0)
    return x + _dot((u * s).astype(BF), wout_ref[...])


def _stage_kernel(pre, ple, want_q, split_in, split_out, *refs):
    refs = list(refs)
    take = lambda n: [refs.pop(0) for _ in range(n)]
    x_refs = take(2 if split_in else 1)
    pre_refs = take({None: 0, "gmlp": 6, "oproj": 3}[pre])
    g_ref, w1_ref, w3_ref, w2_ref = take(4)
    ple_refs = take(5 if ple else 0)
    q_refs = take(3 if want_q else 0)
    o_refs = take(2 if split_out else 1)
    vs_ref = take(1)[0] if pre == "gmlp" else None
    q_ref = take(1)[0] if want_q else None
    xn_ref, acc_ref = take(2)
    h_ref = take(1)[0] if pre else None

    x = _load_tok(*x_refs) if split_in else x_refs[0][...]
    if pre == "gmlp":
        x = _gmlp_tile(x, *pre_refs, vs_ref)
    elif pre == "oproj":
        op_ref, os_ref, wo_ref = pre_refs
        o = jnp.where(pl.program_id(0) == NPT, os_ref[...].astype(BF), op_ref[...])
        x = x + _dot(o, wo_ref[...])
    if pre:
        h_ref[...] = x

    def chunk(f0, width):
        xn = xn_ref[...]
        h1 = _dot(xn, w1_ref[:, f0:f0 + width])
        h3 = _dot(xn, w3_ref[:, f0:f0 + width])
        a = (h1 * jax.nn.sigmoid(h1) * h3).astype(BF)
        return _dot(a, w2_ref[f0:f0 + width, :])

    xn_ref[...] = _rms(x, g_ref[...]).astype(BF)
    acc_ref[...] = chunk(0, FFN_EDGE)
    for f0 in range(FFN_EDGE, DFF - FFN_EDGE, FFN_TF):
        acc_ref[...] += chunk(f0, FFN_TF)
    resid = h_ref[...] if pre else (_load_tok(*x_refs) if split_in else x_refs[0][...])
    y = resid + 0.5 * (acc_ref[...] + chunk(DFF - FFN_EDGE, FFN_EDGE))

    if ple:
        gp_ref, wg_ref, pep_ref, pes_ref, wp_ref = ple_refs
        gate = jax.nn.sigmoid(_dot(_rms(y, gp_ref[...]).astype(BF), wg_ref[...]))
        y = y + gate * _dot(_load_tok(pep_ref, pes_ref).astype(BF), wp_ref[...])
    if split_out:
        _store_tok(*o_refs, y)
    else:
        o_refs[0][...] = y
    if want_q:
        gq_ref, wq_ref, qn_ref = q_refs
        q_ref[...] = _head_rms(_dot(_rms(y, gq_ref[...]).astype(BF), wq_ref[...]), qn_ref[...]) * QSCALE


def _stage(x, ffn, pre=None, pre_args=(), ple_args=None, q_args=None, split_out=False):
    split_in = isinstance(x, tuple)
    tok = pl.BlockSpec((TM, D), lambda i: (i, 0))
    const = lambda i: (0, 0)
    args = list(x) if split_in else [x]
    in_specs = _tok_specs(D) if split_in else [tok]
    if pre == "gmlp":
        g, w_in, v_norm, wmix, bmix, w_out = pre_args
        args += [g, w_in, v_norm, wmix, bmix, w_out]
        in_specs += [_resident((1, D)), _resident((D, 2 * D)), _resident((1, D)),
                     pl.BlockSpec((None, NH, DH, DH), lambda i: (i // NPT, 0, 0, 0)),
                     pl.BlockSpec((None, DH, D), lambda i: (i // NPT, 0, 0)),
                     _resident((D, D))]
    elif pre == "oproj":
        args += list(pre_args)
        in_specs += _tok_specs(D) + [_resident((D, D))]
    args += list(ffn)
    in_specs += [_resident((1, D)), _resident((D, DFF)), _resident((D, DFF)), _resident((DFF, D))]
    if ple_args is not None:
        g, w_gate, pe_p, pe_s, layer, w_proj = ple_args
        args += [g, w_gate, pe_p, pe_s, w_proj]
        in_specs += [_resident((1, D)), _resident((D, D)),
                     pl.BlockSpec((None, TM, PLE), lambda i: (layer, jnp.minimum(i, NPT - 1), 0)),
                     pl.BlockSpec((None, NS, PLE), lambda i: (layer, 0, 0)),
                     _resident((PLE, D))]
    if q_args is not None:
        args += list(q_args)
        in_specs += [_resident((1, D)), _resident((D, D)), _resident((1, D))]
    out_specs = _tok_specs(D) if split_out else [tok]
    out_shape = _tok_shapes(D, F32) if split_out else [jax.ShapeDtypeStruct((N, D), F32)]
    if pre == "gmlp":
        out_specs.append(pl.BlockSpec((NS, D), const))
        out_shape.append(jax.ShapeDtypeStruct((NS, D), F32))
    if q_args is not None:
        out_specs.append(tok)
        out_shape.append(jax.ShapeDtypeStruct((N, D), F32))
    scratch = [pltpu.VMEM((TM, D), BF), pltpu.VMEM((TM, D), F32)]
    if pre:
        scratch.append(pltpu.VMEM((TM, D), F32))
    out = pl.pallas_call(
        functools.partial(_stage_kernel, pre, ple_args is not None, q_args is not None, split_in, split_out),
        grid=(N_TILES,),
        in_specs=in_specs,
        out_specs=out_specs,
        out_shape=out_shape,
        scratch_shapes=scratch,
        compiler_params=_params("arbitrary"),
        name="stage",
    )(*args)
    if split_out:
        return ((out[0], out[1]),) + tuple(out[2:])
    return tuple(out)


def _kv_kernel(x_ref, g_ref, wk_ref, wv_ref, wvt_ref, wf_ref, bf_ref, kn_ref,
               k32p_ref, k32s_ref, v32p_ref, v32s_ref, kb_ref, vt_ref, a_ref, lf_ref, cs_ref, carry_ref):
    i = pl.program_id(0)
    is_sample = i == NPT
    xs = _rms(x_ref[...], g_ref[...]).astype(BF)
    k = _head_rms(_dot(xs, wk_ref[...]), kn_ref[...])
    _store_tok(k32p_ref, k32s_ref, k)
    _store_tok(v32p_ref, v32s_ref, _dot(xs, wv_ref[...]))

    f = _dot_nt(wf_ref[...], xs)[:NH] + bf_ref[...]
    lf = jnp.minimum(f, 0.0) - jnp.log1p(jnp.exp(-jnp.abs(f)))
    lf_ref[...] = lf

    lane = lax.broadcasted_iota(jnp.int32, (NH, TM), 1)
    pos = lane & jnp.where(is_sample, DEC_T - 1, TM - 1)
    y = lf
    sh = 1
    while sh < TM:
        y = y + jnp.where(pos >= sh, pltpu.roll(y, sh, 1), 0.0)
        sh *= 2

    @pl.when(jnp.logical_or(i % (SEQ // TM) == 0, is_sample))
    def _():
        carry_ref[...] = jnp.zeros_like(carry_ref)

    y = y + carry_ref[:, 0:1]
    cs_ref[...] = y
    carry_ref[...] = jnp.broadcast_to(y[:, TM - 1:TM], carry_ref.shape)

    @pl.when(i < NPT)
    def _():
        kb_ref[...] = k.astype(BF)
        vt_ref[...] = _dot_nt(wvt_ref[...], xs).astype(BF)
        hi, mid, lo = _split3(-(y * LOG2E))
        row = lax.broadcasted_iota(jnp.int32, (NH, TM), 0)
        ones = jnp.where(row < 3, 1.0, 0.0)
        pad = jnp.zeros((DH - 4 * NH, TM), F32)
        a_ref[...] = jnp.transpose(jnp.concatenate([hi, mid, lo, ones, pad], axis=0)).astype(BF)


def _kv(h, g, wk, wv, wvt, wf, bf, kn):
    const = lambda i: (0, 0)
    ptile = lambda i: (jnp.minimum(i, NPT - 1), 0)
    return pl.pallas_call(
        _kv_kernel,
        grid=(N_TILES,),
        in_specs=[
            pl.BlockSpec((TM, D), lambda i: (i, 0)),
            pl.BlockSpec((1, D), const),
            pl.BlockSpec((D, D), const),
            pl.BlockSpec((D, D), const),
            pl.BlockSpec((D, D), const),
            pl.BlockSpec((16, D), const),
            pl.BlockSpec((NH, 1), const),
            pl.BlockSpec((1, D), const),
        ],
        out_specs=_tok_specs(D) + _tok_specs(D) + [
            pl.BlockSpec((TM, D), ptile),
            pl.BlockSpec((D, TM), lambda i: (0, jnp.minimum(i, NPT - 1))),
            pl.BlockSpec((TM, DH), ptile),
            pl.BlockSpec((NH, TM), lambda i: (0, i)),
            pl.BlockSpec((NH, TM), lambda i: (0, i)),
        ],
        out_shape=_tok_shapes(D, F32) + _tok_shapes(D, F32) + [
            jax.ShapeDtypeStruct((NP, D), BF),
            jax.ShapeDtypeStruct((D, NP), BF),
            jax.ShapeDtypeStruct((NP, DH), BF),
            jax.ShapeDtypeStruct((NH, N), F32),
            jax.ShapeDtypeStruct((NH, N), F32),
        ],
        scratch_shapes=[pltpu.VMEM((NH, 128), F32)],
        compiler_params=_params("arbitrary"),
        name="shared_kv",
    )(h, g, wk, wv, wvt, wf, bf, kn)


def _attn_prompt_kernel(q_ref, k_ref, a_ref, vt_ref, c_ref, o_ref):
    h = pl.program_id(1)
    row = lax.broadcasted_iota(jnp.int32, (DH, TQ), 0)
    own = jnp.logical_and(row < 3 * NH, (row & (NH - 1)) == h)
    key = lax.broadcasted_iota(jnp.int32, (TQ, TQ), 0)
    qry = lax.broadcasted_iota(jnp.int32, (TQ, TQ), 1)

    def query_block(qi):
        q0 = qi * TQ
        hi, mid, lo = _split3(c_ref[0, :, q0:q0 + TQ] * LOG2E)
        aug = jnp.where(row == 3 * NH, hi, jnp.where(row == 3 * NH + 1, mid, jnp.where(row == 3 * NH + 2, lo, 0.0)))
        aug = jnp.where(own, 1.0, aug)
        return jnp.concatenate([jnp.transpose(q_ref[q0:q0 + TQ, :]), aug], axis=0).astype(BF)

    def scores(qa, qi, g):
        k0 = g * TQ
        kaug = jnp.concatenate([k_ref[k0:k0 + TQ, :], a_ref[k0:k0 + TQ, :]], axis=1)
        s = _dot(kaug, qa)
        return jnp.where(key <= qry, s, -jnp.inf) if g == qi else s

    pairs = [(qi, g) for qi in range(SEQ // TQ) for g in range(qi + 1)]
    qa = None
    queue = []

    def issue_scores(n):
        nonlocal qa
        if n < len(pairs):
            qi1, g1 = pairs[n]
            if g1 == 0:
                qa = query_block(qi1)
            queue.append(scores(qa, qi1, g1))

    for n in range(SCORE_LOOKAHEAD):
        issue_scores(n)
    m = l = acc = pending = None
    for n, (qi, g) in enumerate(pairs):
        issue_scores(n + SCORE_LOOKAHEAD)
        s = queue.pop(0)
        m_cur = jnp.max(s, axis=0, keepdims=True)
        if g == 0:
            m, alpha = m_cur, None
            p = jnp.exp2(s - m)
            l = jnp.sum(p, axis=0, keepdims=True)
        else:
            m_new = jnp.maximum(m, m_cur)
            alpha = jnp.exp2(m - m_new)
            p = jnp.exp2(s - m_new)
            l = alpha * l + jnp.sum(p, axis=0, keepdims=True)
            m = m_new
        if pending is not None:
            acc = pending()
        k0 = g * TQ

        def pending(acc=acc, alpha=alpha, p=p, k0=k0, qi=qi, g=g, l=l):
            pv = _dot(vt_ref[:, k0:k0 + TQ], p.astype(BF))
            new = pv if alpha is None else alpha * acc + pv
            if g == qi:
                o_ref[qi * TQ:(qi + 1) * TQ, :] = jnp.transpose(new / l).astype(BF)
            return new
    pending()


def _attn_prompt(q, kb, a, vt, cs3):
    return pl.pallas_call(
        _attn_prompt_kernel,
        grid=(BATCH, NH),
        in_specs=[
            pl.BlockSpec((SEQ, DH), lambda b, h: (b, h)),
            pl.BlockSpec((SEQ, DH), lambda b, h: (b, h)),
            pl.BlockSpec((SEQ, DH), lambda b, h: (b, 0)),
            pl.BlockSpec((DH, SEQ), lambda b, h: (h, b)),
            pl.BlockSpec((1, 1, SEQ), lambda b, h: (h, 0, b)),
        ],
        out_specs=pl.BlockSpec((SEQ, DH), lambda b, h: (b, h)),
        out_shape=jax.ShapeDtypeStruct((NP, D), BF),
        compiler_params=_params("parallel", "parallel"),
        name="attn_prompt",
    )(q, kb, a, vt, cs3)


def _suffix_kernel(x_ref, o_ref):
    x = x_ref[...]
    width = PAGE * NH
    lane = lax.broadcasted_iota(jnp.int32, x.shape, 1)
    y = x
    t = x
    sh = NH
    while sh < width:
        y = y + jnp.where(lane < width - sh, pltpu.roll(y, width - sh, 1), 0.0)
        t = t + pltpu.roll(t, sh, 1)
        sh *= 2
    o_ref[:, :width] = y - x
    o_ref[:, width:] = t


def _suffix(lf_rows):
    rows, width = lf_rows.shape
    tr = 256
    spec = pl.BlockSpec((tr, width), lambda i: (i, 0))
    return pl.pallas_call(
        _suffix_kernel,
        grid=(rows // tr,),
        in_specs=[spec],
        out_specs=pl.BlockSpec((tr, 2 * width), lambda i: (i, 0)),
        out_shape=jax.ShapeDtypeStruct((rows, 2 * width), F32),
        compiler_params=_params("parallel"),
        name="cache_suffix",
    )(lf_rows)


def _attn_sample_kernel(pt_ref, q_ref, kn_ref, vn_ref, cq_ref, ckn_ref, ck_hbm, cv_hbm, f_hbm, o_ref,
                        kbuf, vbuf, fbuf, sems, m_ref, l_ref, acc_ref, carry_ref):
    pp = PAGES_PER_STEP
    steps_per_seq = N_PAGES // pp
    n_steps = DEC_B * steps_per_seq
    b = pl.program_id(0)
    p = pl.program_id(1)
    t = b * steps_per_seq + p

    def page_copies(step, slot):
        sb = step // steps_per_seq
        sp = step % steps_per_seq
        copies = []
        for j in range(pp):
            page = pt_ref[sb, N_PAGES - 1 - (pp * sp + j)]
            copies.append(pltpu.make_async_copy(ck_hbm.at[page], kbuf.at[slot, j], sems.at[0, slot]))
            copies.append(pltpu.make_async_copy(cv_hbm.at[page], vbuf.at[slot, j], sems.at[1, slot]))
            copies.append(pltpu.make_async_copy(f_hbm.at[page], fbuf.at[slot, j], sems.at[2, slot]))
        return copies

    def fetch(step, slot):
        for c in page_copies(step, slot):
            c.start()

    @pl.when(t == 0)
    def _():
        fetch(0, 0)
        fetch(1, 1)

    @pl.when(t + 2 < n_steps)
    def _():
        fetch(t + 2, (t + 2) % PAGE_SLOTS)

    slot = t % PAGE_SLOTS
    for c in page_copies(t, slot):
        c.wait()

    q = q_ref[0]
    cq = cq_ref[0][:, 0:1] * LOG2E

    @pl.when(p == 0)
    def _():
        s = _dot_nt(q, kn_ref[0])
        s = s + cq - ckn_ref[0] * LOG2E
        row = lax.broadcasted_iota(jnp.int32, s.shape, 0)
        col = lax.broadcasted_iota(jnp.int32, s.shape, 1)
        ok = jnp.logical_and((col & (NH - 1)) == (row & (NH - 1)), (col >> 3) <= (row >> 3))
        s = jnp.where(ok, s, -jnp.inf)
        m = jnp.max(s, axis=1, keepdims=True)
        e = jnp.exp2(s - m)
        m_ref[...] = m
        l_ref[...] = jnp.sum(e, axis=1, keepdims=True)
        acc_ref[...] = _dot(e.astype(BF), vn_ref[0])
        carry_ref[...] = jnp.zeros_like(carry_ref)

    carry = carry_ref[...]
    tails = []
    for j in range(pp):
        tails.append(fbuf[slot, j, :, :PAGE * NH] + carry)
        carry = carry + fbuf[slot, j, :, PAGE * NH:]
    carry_ref[...] = carry
    tail = jnp.concatenate(tails, axis=1) * LOG2E
    k = jnp.concatenate([kbuf[slot, j].reshape(PAGE * NH, DH).astype(BF) for j in range(pp)], axis=0)
    v = jnp.concatenate([vbuf[slot, j].reshape(PAGE * NH, DH).astype(BF) for j in range(pp)], axis=0)
    s = _dot_nt(q, k)
    row = lax.broadcasted_iota(jnp.int32, s.shape, 0)
    col = lax.broadcasted_iota(jnp.int32, s.shape, 1)
    s = jnp.where((col & (NH - 1)) == (row & (NH - 1)), s + cq + tail, -jnp.inf)
    m_prev = m_ref[...]
    m_new = jnp.maximum(m_prev, jnp.max(s, axis=1, keepdims=True))
    alpha = jnp.exp2(m_prev - m_new)
    e = jnp.exp2(s - m_new)
    l_ref[...] = alpha * l_ref[...] + jnp.sum(e, axis=1, keepdims=True)
    acc_ref[...] = alpha * acc_ref[...] + _dot(e.astype(BF), v)
    m_ref[...] = m_new

    @pl.when(p == steps_per_seq - 1)
    def _():
        o_ref[0] = acc_ref[...] / l_ref[...]


def _attn_sample(page_table, q, kn, vn, cq, ckn, cache_k, cache_v, forget):
    pp = PAGES_PER_STEP

    def seq(b, p, pt):
        return (b, 0, 0)

    hbm = pl.BlockSpec(memory_space=pl.ANY)
    grid_spec = pltpu.PrefetchScalarGridSpec(
        num_scalar_prefetch=1,
        grid=(DEC_B, N_PAGES // pp),
        in_specs=[
            pl.BlockSpec((1, ROWS, DH), seq),
            pl.BlockSpec((1, 128, DH), seq),
            pl.BlockSpec((1, 128, DH), seq),
            pl.BlockSpec((1, ROWS, 128), seq),
            pl.BlockSpec((1, 1, 128), seq),
            hbm, hbm, hbm,
        ],
        out_specs=pl.BlockSpec((1, ROWS, DH), seq),
        scratch_shapes=[
            pltpu.VMEM((PAGE_SLOTS, pp, PAGE, NH, DH), F32),
            pltpu.VMEM((PAGE_SLOTS, pp, PAGE, NH, DH), F32),
            pltpu.VMEM((PAGE_SLOTS, pp, 1, 2 * PAGE * NH), F32),
            pltpu.SemaphoreType.DMA((3, PAGE_SLOTS)),
            pltpu.VMEM((ROWS, 1), F32),
            pltpu.VMEM((ROWS, 1), F32),
            pltpu.VMEM((ROWS, DH), F32),
            pltpu.VMEM((1, PAGE * NH), F32),
        ],
    )
    return pl.pallas_call(
        _attn_sample_kernel,
        grid_spec=grid_spec,
        out_shape=jax.ShapeDtypeStruct((DEC_B, ROWS, DH), F32),
        compiler_params=_params("arbitrary", "arbitrary"),
        name="attn_sample",
    )(page_table, q, kn, vn, cq, ckn, cache_k, cache_v, forget)


def kernel(x_prompt, x_sample, p_prompt, p_sample, cache_k, cache_v, cache_logf, page_table,
           ffn1_norm, ffn1_w1, ffn1_w3, ffn1_w2, mix_norm, ffn2_norm, ffn2_w1, ffn2_w3, ffn2_w2,
           ple_norm, ple_w_gate, ple_w_proj, gmlp_w_in, gmlp_v_norm, gmlp_w_s, gmlp_b_s, gmlp_w_out,
           kv_norm, w_kvf, b_f, k_norm, att_w_q, q_norm, att_w_o):
    bf = lambda w: w.astype(BF)
    row = lambda g: g.reshape(1, -1)

    h = (x_prompt.reshape(NP, D), x_sample.reshape(NS, D))
    pe_p, pe_s = p_prompt.reshape(DEPTH, NP, PLE), p_sample.reshape(DEPTH, NS, PLE)

    tril = jnp.tril(jnp.ones((DH, DH), F32))
    w_prompt = gmlp_w_s * tril
    small = (gmlp_w_s[:, :, :DEC_T, :DEC_T] * tril[:DEC_T, :DEC_T])
    eye = jnp.eye(DH // DEC_T, dtype=F32)
    w_sample = jnp.einsum('ab,lgts->lgatbs', eye, small).reshape(N_A, NH, DH, DH)
    wmix = bf(jnp.stack([w_prompt, w_sample], axis=1))
    b_prompt = jnp.repeat(gmlp_b_s.transpose(0, 2, 1), DH, axis=2)
    b_sample = jnp.tile(b_prompt[:, :DEC_T], (1, DH // DEC_T, 1))
    bmix = jnp.stack([b_prompt, b_sample], axis=1)

    wk = bf(w_kvf[:, :D])
    wv = bf(w_kvf[:, D:2 * D])
    wf = bf(jnp.pad(w_kvf[:, 2 * D:].T, ((0, 16 - NH), (0, 0))))

    kv = None
    gmlp_vs = []
    for i in range(DEPTH):
        ffn1 = (row(ffn1_norm[i]), bf(ffn1_w1[i]), bf(ffn1_w3[i]), bf(ffn1_w2[i]))
        ffn2 = (row(ffn2_norm[i]), bf(ffn2_w1[i]), bf(ffn2_w3[i]), bf(ffn2_w2[i]))
        ple_args = (row(ple_norm[i]), bf(ple_w_gate[i]), pe_p, pe_s, i, bf(ple_w_proj[i]))
        if i < N_A:
            (h,) = _stage(h, ffn1)
            gmlp_args = (row(mix_norm[i]), bf(gmlp_w_in[i]), row(gmlp_v_norm[i]), wmix[i], bmix[i], bf(gmlp_w_out[i]))
            h, v_rows = _stage(h, ffn2, pre="gmlp", pre_args=gmlp_args, ple_args=ple_args)
            gmlp_vs.append(v_rows.reshape(DEC_B, DEC_T, D))
        else:
            j = i - N_A
            k32s, v32s, kb, vt, a, cst, forget = kv
            h, q = _stage(h, ffn1, q_args=(row(mix_norm[i]), bf(att_w_q[j]), row(jnp.tile(q_norm[j], NH))))
            o_p = _attn_prompt(q, kb, a, vt, cst.reshape(NH, 1, N))
            heads = lambda x: bf(x).reshape(DEC_B, ROWS, DH)
            pad_rows = lambda x: jnp.pad(x, ((0, 0), (0, 128 - ROWS), (0, 0)))
            c_new = cst[:, NP:].reshape(NH, DEC_B, DEC_T).transpose(1, 2, 0).reshape(DEC_B, ROWS)
            cq = jnp.broadcast_to(c_new[:, :, None], (DEC_B, ROWS, 128))
            ckn = jnp.pad(c_new, ((0, 0), (0, 128 - ROWS)))[:, None, :]
            o_s = _attn_sample(page_table, heads(q[NP:]), pad_rows(heads(k32s)), pad_rows(heads(v32s)),
                               cq, ckn, cache_k, cache_v, forget)
            (h,) = _stage(h, ffn2, pre="oproj", pre_args=(o_p, o_s.reshape(NS, D), bf(att_w_o[j])),
                          ple_args=ple_args, split_out=i == DEPTH - 1)
        if i == N_A - 1:
            k32p, k32s, v32p, v32s, kb, vt, a, lft, cst = _kv(
                h, row(kv_norm), wk, wv, bf(w_kvf[:, D:2 * D].T), wf,
                b_f.reshape(NH, 1), row(jnp.tile(k_norm, NH)))
            n_pool = cache_logf.shape[0]
            forget = _suffix(cache_logf.reshape(n_pool, PAGE * NH)).reshape(n_pool, 1, 2 * PAGE * NH)
            kv = (k32s, v32s, kb, vt, a, cst, forget)

    hp, hs = h
    lf_tok = lft.T
    return (
        hp.reshape(BATCH, SEQ, D),
        hs.reshape(DEC_B, DEC_T, D),
        k32p.reshape(BATCH, SEQ, NH, DH),
        v32p.reshape(BATCH, SEQ, NH, DH),
        lf_tok[:NP].reshape(BATCH, SEQ, NH),
        k32s.reshape(DEC_B, DEC_T, NH, DH),
        v32s.reshape(DEC_B, DEC_T, NH, DH),
        lf_tok[NP:].reshape(DEC_B, DEC_T, NH),
        jnp.stack(gmlp_vs),
    )
```

```python
import functools

import jax
import jax.numpy as jnp
from jax import lax
from jax.experimental import pallas as pl
from jax.experimental.pallas import tpu as pltpu

D = 1024
DFF = 2816
NH = 8
DH = 128
PLE = 256
DEPTH = 4
N_A = 2
BATCH, SEQ = 4, 4096
DEC_B, DEC_T = 128, 4
PAGE = 128
N_PAGES = 16
NP = BATCH * SEQ
NS = DEC_B * DEC_T
N = NP + NS
EPS = 1e-6
LOG2E = 1.4426950408889634
QSCALE = DH ** -0.5 * LOG2E

BF = jnp.bfloat16
F32 = jnp.float32

TM = 512
NPT = NP // TM
N_TILES = NPT + 1
FFN_CHUNKS = (256, 768, 768, 768, 256)
TQ = 512
SCORE_LOOKAHEAD = 2
PAGES_PER_STEP = 8
PAGE_SLOTS = 3
ROWS = DEC_T * NH
VMEM_LIMIT = 56 * 1024 * 1024

NT_DIMS = (((1,), (1,)), ((), ()))


def _params(*sem):
    return pltpu.CompilerParams(dimension_semantics=sem, vmem_limit_bytes=VMEM_LIMIT)


def _resident(shape):
    return pl.BlockSpec(shape, lambda *_: (0,) * len(shape), pipeline_mode=pl.Buffered(1))


def _tok_specs(width):
    return [pl.BlockSpec((TM, width), lambda i: (jnp.minimum(i, NPT - 1), 0)),
            pl.BlockSpec((NS, width), lambda i: (0, 0))]


def _tok_shapes(width, dtype):
    return [jax.ShapeDtypeStruct((NP, width), dtype), jax.ShapeDtypeStruct((NS, width), dtype)]


def _load_tok(p_ref, s_ref):
    return jnp.where(pl.program_id(0) == NPT, s_ref[...], p_ref[...])


def _store_tok(p_ref, s_ref, y):
    i = pl.program_id(0)

    @pl.when(i < NPT)
    def _():
        p_ref[...] = y

    @pl.when(i == NPT)
    def _():
        s_ref[...] = y


def _rms(x, g):
    ms = jnp.mean(x * x, axis=-1, keepdims=True)
    return x * lax.rsqrt(ms + EPS) * g


def _head_rms(x, g):
    outs = []
    for h in range(NH):
        xh = x[:, h * DH:(h + 1) * DH]
        ms = jnp.mean(xh * xh, axis=-1, keepdims=True)
        outs.append(xh * lax.rsqrt(ms + EPS))
    return jnp.concatenate(outs, axis=-1) * g


def _dot(a, b):
    return jnp.dot(a, b, preferred_element_type=F32)


def _dot_nt(a, b):
    return lax.dot_general(a, b, NT_DIMS, preferred_element_type=F32)


def _split3(x):
    hi = x.astype(BF).astype(F32)
    r = x - hi
    mid = r.astype(BF).astype(F32)
    return hi, mid, r - mid


def _gmlp_tile(x, g_ref, win_ref, vn_ref, wmix_ref, bmix_ref, wout_ref, vs_ref):
    xn = _rms(x, g_ref[...]).astype(BF)
    uv = _dot(xn, win_ref[...])
    u = uv[:, :D]
    v = _rms(uv[:, D:], vn_ref[...])

    @pl.when(pl.program_id(0) == NPT)
    def _():
        vs_ref[...] = v

    vb = v.astype(BF)
    bias = bmix_ref[...]
    rows = []
    for c in range(TM // DH):
        cols = []
        for g in range(NH):
            blk = vb[c * DH:(c + 1) * DH, g * DH:(g + 1) * DH]
            cols.append(_dot(wmix_ref[g], blk))
        rows.append(jnp.concatenate(cols, axis=1) + bias)
    s = jnp.concatenate(rows, axis=0)
    return x + _dot((u * s).astype(BF), wout_ref[...])


def _stage_kernel(pre, ple, want_q, split_in, split_out, *refs):
    refs = list(refs)
    take = lambda n: [refs.pop(0) for _ in range(n)]
    x_refs = take(2 if split_in else 1)
    pre_refs = take({None: 0, "gmlp": 6, "oproj": 3}[pre])
    g_ref, w1_ref, w3_ref, w2_ref = take(4)
    ple_refs = take(5 if ple else 0)
    q_refs = take(3 if want_q else 0)
    o_refs = take(2 if split_out else 1)
    vs_ref = take(1)[0] if pre == "gmlp" else None
    q_ref = take(1)[0] if want_q else None
    xn_ref, acc_ref = take(2)
    h_ref = take(1)[0] if pre else None

    x = _load_tok(*x_refs) if split_in else x_refs[0][...]
    if pre == "gmlp":
        x = _gmlp_tile(x, *pre_refs, vs_ref)
    elif pre == "oproj":
        op_ref, os_ref, wo_ref = pre_refs
        o = jnp.where(pl.program_id(0) == NPT, os_ref[...].astype(BF), op_ref[...])
        x = x + _dot(o, wo_ref[...])
    if pre:
        h_ref[...] = x

    def chunk(f0, width):
        xn = xn_ref[...]
        h1 = _dot(xn, w1_ref[:, f0:f0 + width])
        h3 = _dot(xn, w3_ref[:, f0:f0 + width])
        a = (h1 * jax.nn.sigmoid(h1) * h3).astype(BF)
        return _dot(a, w2_ref[f0:f0 + width, :])

    xn_ref[...] = _rms(x, g_ref[...]).astype(BF)
    starts = [sum(FFN_CHUNKS[:c]) for c in range(len(FFN_CHUNKS))]
    acc_ref[...] = chunk(starts[0], FFN_CHUNKS[0])
    for f0, width in zip(starts[1:-1], FFN_CHUNKS[1:-1]):
        acc_ref[...] += chunk(f0, width)
    resid = h_ref[...] if pre else (_load_tok(*x_refs) if split_in else x_refs[0][...])
    y = resid + 0.5 * (acc_ref[...] + chunk(starts[-1], FFN_CHUNKS[-1]))

    if ple:
        gp_ref, wg_ref, pep_ref, pes_ref, wp_ref = ple_refs
        gate = jax.nn.sigmoid(_dot(_rms(y, gp_ref[...]).astype(BF), wg_ref[...]))
        y = y + gate * _dot(_load_tok(pep_ref, pes_ref).astype(BF), wp_ref[...])
    if split_out:
        _store_tok(*o_refs, y)
    else:
        o_refs[0][...] = y
    if want_q:
        gq_ref, wq_ref, qn_ref = q_refs
        q_ref[...] = _head_rms(_dot(_rms(y, gq_ref[...]).astype(BF), wq_ref[...]), qn_ref[...]) * QSCALE


def _stage(x, ffn, pre=None, pre_args=(), ple_args=None, q_args=None, split_out=False):
    split_in = isinstance(x, tuple)
    tok = pl.BlockSpec((TM, D), lambda i: (i, 0))
    const = lambda i: (0, 0)
    args = list(x) if split_in else [x]
    in_specs = _tok_specs(D) if split_in else [tok]
    if pre == "gmlp":
        g, w_in, v_norm, wmix, bmix, w_out = pre_args
        args += [g, w_in, v_norm, wmix, bmix, w_out]
        in_specs += [_resident((1, D)), _resident((D, 2 * D)), _resident((1, D)),
                     pl.BlockSpec((None, NH, DH, DH), lambda i: (i // NPT, 0, 0, 0)),
                     pl.BlockSpec((None, DH, D), lambda i: (i // NPT, 0, 0)),
                     _resident((D, D))]
    elif pre == "oproj":
        args += list(pre_args)
        in_specs += _tok_specs(D) + [_resident((D, D))]
    args += list(ffn)
    in_specs += [_resident((1, D)), _resident((D, DFF)), _resident((D, DFF)), _resident((DFF, D))]
    if ple_args is not None:
        g, w_gate, pe_p, pe_s, layer, w_proj = ple_args
        args += [g, w_gate, pe_p, pe_s, w_proj]
        in_specs += [_resident((1, D)), _resident((D, D)),
                     pl.BlockSpec((None, TM, PLE), lambda i: (layer, jnp.minimum(i, NPT - 1), 0)),
                     pl.BlockSpec((None, NS, PLE), lambda i: (layer, 0, 0)),
                     _resident((PLE, D))]
    if q_args is not None:
        args += list(q_args)
        in_specs += [_resident((1, D)), _resident((D, D)), _resident((1, D))]
    out_specs = _tok_specs(D) if split_out else [tok]
    out_shape = _tok_shapes(D, F32) if split_out else [jax.ShapeDtypeStruct((N, D), F32)]
    if pre == "gmlp":
        out_specs.append(pl.BlockSpec((NS, D), const))
        out_shape.append(jax.ShapeDtypeStruct((NS, D), F32))
    if q_args is not None:
        out_specs.append(tok)
        out_shape.append(jax.ShapeDtypeStruct((N, D), F32))
    scratch = [pltpu.VMEM((TM, D), BF), pltpu.VMEM((TM, D), F32)]
    if pre:
        scratch.append(pltpu.VMEM((TM, D), F32))
    out = pl.pallas_call(
        functools.partial(_stage_kernel, pre, ple_args is not None, q_args is not None, split_in, split_out),
        grid=(N_TILES,),
        in_specs=in_specs,
        out_specs=out_specs,
        out_shape=out_shape,
        scratch_shapes=scratch,
        compiler_params=_params("arbitrary"),
        name="stage",
    )(*args)
    if split_out:
        return ((out[0], out[1]),) + tuple(out[2:])
    return tuple(out)


def _kv_kernel(x_ref, g_ref, wk_ref, wv_ref, wf_ref, bf_ref, kn_ref,
               k32p_ref, k32s_ref, v32p_ref, v32s_ref, kb_ref, vt_ref, a_ref, lf_ref, cs_ref, carry_ref):
    i = pl.program_id(0)
    is_sample = i == NPT
    xs = _rms(x_ref[...], g_ref[...]).astype(BF)
    k = _head_rms(_dot(xs, wk_ref[...]), kn_ref[...])
    v = _dot(xs, wv_ref[...])
    kb_ref[...] = k.astype(BF)
    vt_ref[...] = jnp.transpose(v).astype(BF)

    f = _dot_nt(wf_ref[...], xs)[:NH] + bf_ref[...]
    lf = jnp.minimum(f, 0.0) - jnp.log1p(jnp.exp(-jnp.abs(f)))
    lf_ref[...] = lf

    lane = lax.broadcasted_iota(jnp.int32, (NH, TM), 1)
    pos = lane & jnp.where(is_sample, DEC_T - 1, TM - 1)
    y = lf
    sh = 1
    while sh < TM:
        y = y + jnp.where(pos >= sh, pltpu.roll(y, sh, 1), 0.0)
        sh *= 2

    @pl.when(jnp.logical_or(i % (SEQ // TM) == 0, is_sample))
    def _():
        carry_ref[...] = jnp.zeros_like(carry_ref)

    y = y + carry_ref[:, 0:1]
    cs_ref[...] = y
    carry_ref[...] = jnp.broadcast_to(y[:, TM - 1:TM], carry_ref.shape)

    hi, mid, lo = _split3(-(y * LOG2E))
    row = lax.broadcasted_iota(jnp.int32, (NH, TM), 0)
    ones = jnp.where(row < 3, 1.0, 0.0)
    pad = jnp.zeros((DH - 4 * NH, TM), F32)
    a_ref[...] = jnp.transpose(jnp.concatenate([hi, mid, lo, ones, pad], axis=0)).astype(BF)

    _store_tok(k32p_ref, k32s_ref, k)
    _store_tok(v32p_ref, v32s_ref, v)


def _kv(h, g, wk, wv, wf, bf, kn):
    const = lambda i: (0, 0)
    tile = lambda i: (i, 0)
    return pl.pallas_call(
        _kv_kernel,
        grid=(N_TILES,),
        in_specs=[
            pl.BlockSpec((TM, D), lambda i: (i, 0)),
            pl.BlockSpec((1, D), const),
            pl.BlockSpec((D, D), const),
            pl.BlockSpec((D, D), const),
            pl.BlockSpec((16, D), const),
            pl.BlockSpec((NH, 1), const),
            pl.BlockSpec((1, D), const),
        ],
        out_specs=_tok_specs(D) + _tok_specs(D) + [
            pl.BlockSpec((TM, D), tile),
            pl.BlockSpec((D, TM), lambda i: (0, i)),
            pl.BlockSpec((TM, DH), tile),
            pl.BlockSpec((NH, TM), lambda i: (0, i)),
            pl.BlockSpec((NH, TM), lambda i: (0, i)),
        ],
        out_shape=_tok_shapes(D, F32) + _tok_shapes(D, F32) + [
            jax.ShapeDtypeStruct((N, D), BF),
            jax.ShapeDtypeStruct((D, N), BF),
            jax.ShapeDtypeStruct((N, DH), BF),
            jax.ShapeDtypeStruct((NH, N), F32),
            jax.ShapeDtypeStruct((NH, N), F32),
        ],
        scratch_shapes=[pltpu.VMEM((NH, 128), F32)],
        compiler_params=_params("arbitrary"),
        name="shared_kv",
    )(h, g, wk, wv, wf, bf, kn)


def _attn_prompt_kernel(q_ref, k_ref, a_ref, vt_ref, c_ref, o_ref):
    h = pl.program_id(1)
    row = lax.broadcasted_iota(jnp.int32, (DH, TQ), 0)
    own = jnp.logical_and(row < 3 * NH, (row & (NH - 1)) == h)
    key = lax.broadcasted_iota(jnp.int32, (TQ, TQ), 0)
    qry = lax.broadcasted_iota(jnp.int32, (TQ, TQ), 1)

    def query_block(qi):
        q0 = qi * TQ
        hi, mid, lo = _split3(c_ref[0, :, q0:q0 + TQ] * LOG2E)
        aug = jnp.where(row == 3 * NH, hi, jnp.where(row == 3 * NH + 1, mid, jnp.where(row == 3 * NH + 2, lo, 0.0)))
        aug = jnp.where(own, 1.0, aug)
        return jnp.concatenate([jnp.transpose(q_ref[q0:q0 + TQ, :]), aug], axis=0).astype(BF)

    def scores(qa, qi, g):
        k0 = g * TQ
        kaug = jnp.concatenate([k_ref[k0:k0 + TQ, :], a_ref[k0:k0 + TQ, :]], axis=1)
        s = _dot(kaug, qa)
        return jnp.where(key <= qry, s, -jnp.inf) if g == qi else s

    pairs = [(qi, g) for qi in range(SEQ // TQ) for g in range(qi + 1)]
    qa = None
    queue = []

    def issue_scores(n):
        nonlocal qa
        if n < len(pairs):
            qi1, g1 = pairs[n]
            if g1 == 0:
                qa = query_block(qi1)
            queue.append(scores(qa, qi1, g1))

    for n in range(SCORE_LOOKAHEAD):
        issue_scores(n)
    m = l = acc = pending = None
    for n, (qi, g) in enumerate(pairs):
        issue_scores(n + SCORE_LOOKAHEAD)
        s = queue.pop(0)
        m_cur = jnp.max(s, axis=0, keepdims=True)
        if g == 0:
            m, alpha = m_cur, None
            p = jnp.exp2(s - m)
            l = jnp.sum(p, axis=0, keepdims=True)
        else:
            m_new = jnp.maximum(m, m_cur)
            alpha = jnp.exp2(m - m_new)
            p = jnp.exp2(s - m_new)
            l = alpha * l + jnp.sum(p, axis=0, keepdims=True)
            m = m_new
        if pending is not None:
            acc = pending()
        k0 = g * TQ

        def pending(acc=acc, alpha=alpha, p=p, k0=k0, qi=qi, g=g, l=l):
            pv = _dot(vt_ref[:, k0:k0 + TQ], p.astype(BF))
            new = pv if alpha is None else alpha * acc + pv
            if g == qi:
                o_ref[qi * TQ:(qi + 1) * TQ, :] = jnp.transpose(new / l).astype(BF)
            return new
    pending()


def _attn_prompt(q, kb, a, vt, cs3):
    return pl.pallas_call(
        _attn_prompt_kernel,
        grid=(BATCH, NH),
        in_specs=[
            pl.BlockSpec((SEQ, DH), lambda b, h: (b, h)),
            pl.BlockSpec((SEQ, DH), lambda b, h: (b, h)),
            pl.BlockSpec((SEQ, DH), lambda b, h: (b, 0)),
            pl.BlockSpec((DH, SEQ), lambda b, h: (h, b)),
            pl.BlockSpec((1, 1, SEQ), lambda b, h: (h, 0, b)),
        ],
        out_specs=pl.BlockSpec((SEQ, DH), lambda b, h: (b, h)),
        out_shape=jax.ShapeDtypeStruct((NP, D), BF),
        compiler_params=_params("parallel", "parallel"),
        name="attn_prompt",
    )(q, kb, a, vt, cs3)


def _suffix_kernel(x_ref, o_ref):
    x = x_ref[...]
    width = PAGE * NH
    lane = lax.broadcasted_iota(jnp.int32, x.shape, 1)
    y = x
    t = x
    sh = NH
    while sh < width:
        y = y + jnp.where(lane < width - sh, pltpu.roll(y, width - sh, 1), 0.0)
        t = t + pltpu.roll(t, sh, 1)
        sh *= 2
    o_ref[:, :width] = y - x
    o_ref[:, width:] = t


def _suffix(lf_rows):
    rows, width = lf_rows.shape
    tr = 256
    spec = pl.BlockSpec((tr, width), lambda i: (i, 0))
    return pl.pallas_call(
        _suffix_kernel,
        grid=(rows // tr,),
        in_specs=[spec],
        out_specs=pl.BlockSpec((tr, 2 * width), lambda i: (i, 0)),
        out_shape=jax.ShapeDtypeStruct((rows, 2 * width), F32),
        compiler_params=_params("parallel"),
        name="cache_suffix",
    )(lf_rows)


def _attn_sample_kernel(pt_ref, q_ref, kn_ref, vn_ref, cq_ref, ckn_ref, ck_hbm, cv_hbm, f_hbm, o_ref,
                        kbuf, vbuf, fbuf, sems, m_ref, l_ref, acc_ref, carry_ref):
    pp = PAGES_PER_STEP
    steps_per_seq = N_PAGES // pp
    n_steps = DEC_B * steps_per_seq
    b = pl.program_id(0)
    p = pl.program_id(1)
    t = b * steps_per_seq + p

    def page_copies(step, slot):
        sb = step // steps_per_seq
        sp = step % steps_per_seq
        copies = []
        for j in range(pp):
            page = pt_ref[sb, N_PAGES - 1 - (pp * sp + j)]
            copies.append(pltpu.make_async_copy(ck_hbm.at[page], kbuf.at[slot, j], sems.at[0, slot]))
            copies.append(pltpu.make_async_copy(cv_hbm.at[page], vbuf.at[slot, j], sems.at[1, slot]))
            copies.append(pltpu.make_async_copy(f_hbm.at[page], fbuf.at[slot, j], sems.at[2, slot]))
        return copies

    def fetch(step, slot):
        for c in page_copies(step, slot):
            c.start()

    @pl.when(t == 0)
    def _():
        fetch(0, 0)
        fetch(1, 1)

    @pl.when(t + 2 < n_steps)
    def _():
        fetch(t + 2, (t + 2) % PAGE_SLOTS)

    slot = t % PAGE_SLOTS
    for c in page_copies(t, slot):
        c.wait()

    q = q_ref[0]
    cq = cq_ref[0][:, 0:1] * LOG2E

    @pl.when(p == 0)
    def _():
        s = _dot_nt(q, kn_ref[0])
        s = s + cq - ckn_ref[0] * LOG2E
        row = lax.broadcasted_iota(jnp.int32, s.shape, 0)
        col = lax.broadcasted_iota(jnp.int32, s.shape, 1)
        ok = jnp.logical_and((col & (NH - 1)) == (row & (NH - 1)), (col >> 3) <= (row >> 3))
        s = jnp.where(ok, s, -jnp.inf)
        m = jnp.max(s, axis=1, keepdims=True)
        e = jnp.exp2(s - m)
        m_ref[...] = m
        l_ref[...] = jnp.sum(e, axis=1, keepdims=True)
        acc_ref[...] = _dot(e.astype(BF), vn_ref[0])
        carry_ref[...] = jnp.zeros_like(carry_ref)

    carry = carry_ref[...]
    tails = []
    for j in range(pp):
        tails.append(fbuf[slot, j, :, :PAGE * NH] + carry)
        carry = carry + fbuf[slot, j, :, PAGE * NH:]
    carry_ref[...] = carry
    tail = jnp.concatenate(tails, axis=1) * LOG2E
    k = jnp.concatenate([kbuf[slot, j].reshape(PAGE * NH, DH).astype(BF) for j in range(pp)], axis=0)
    v = jnp.concatenate([vbuf[slot, j].reshape(PAGE * NH, DH).astype(BF) for j in range(pp)], axis=0)
    s = _dot_nt(q, k)
    row = lax.broadcasted_iota(jnp.int32, s.shape, 0)
    col = lax.broadcasted_iota(jnp.int32, s.shape, 1)
    s = jnp.where((col & (NH - 1)) == (row & (NH - 1)), s + cq + tail, -jnp.inf)
    m_prev = m_ref[...]
    m_new = jnp.maximum(m_prev, jnp.max(s, axis=1, keepdims=True))
    alpha = jnp.exp2(m_prev - m_new)
    e = jnp.exp2(s - m_new)
    l_ref[...] = alpha * l_ref[...] + jnp.sum(e, axis=1, keepdims=True)
    acc_ref[...] = alpha * acc_ref[...] + _dot(e.astype(BF), v)
    m_ref[...] = m_new

    @pl.when(p == steps_per_seq - 1)
    def _():
        o_ref[0] = acc_ref[...] / l_ref[...]


def _attn_sample(page_table, q, kn, vn, cq, ckn, cache_k, cache_v, forget):
    pp = PAGES_PER_STEP

    def seq(b, p, pt):
        return (b, 0, 0)

    hbm = pl.BlockSpec(memory_space=pl.ANY)
    grid_spec = pltpu.PrefetchScalarGridSpec(
        num_scalar_prefetch=1,
        grid=(DEC_B, N_PAGES // pp),
        in_specs=[
            pl.BlockSpec((1, ROWS, DH), seq),
            pl.BlockSpec((1, 128, DH), seq),
            pl.BlockSpec((1, 128, DH), seq),
            pl.BlockSpec((1, ROWS, 128), seq),
            pl.BlockSpec((1, 1, 128), seq),
            hbm, hbm, hbm,
        ],
        out_specs=pl.BlockSpec((1, ROWS, DH), seq),
        scratch_shapes=[
            pltpu.VMEM((PAGE_SLOTS, pp, PAGE, NH, DH), F32),
            pltpu.VMEM((PAGE_SLOTS, pp, PAGE, NH, DH), F32),
            pltpu.VMEM((PAGE_SLOTS, pp, 1, 2 * PAGE * NH), F32),
            pltpu.SemaphoreType.DMA((3, PAGE_SLOTS)),
            pltpu.VMEM((ROWS, 1), F32),
            pltpu.VMEM((ROWS, 1), F32),
            pltpu.VMEM((ROWS, DH), F32),
            pltpu.VMEM((1, PAGE * NH), F32),
        ],
    )
    return pl.pallas_call(
        _attn_sample_kernel,
        grid_spec=grid_spec,
        out_shape=jax.ShapeDtypeStruct((DEC_B, ROWS, DH), F32),
        compiler_params=_params("arbitrary", "arbitrary"),
        name="attn_sample",
    )(page_table, q, kn, vn, cq, ckn, cache_k, cache_v, forget)


def kernel(x_prompt, x_sample, p_prompt, p_sample, cache_k, cache_v, cache_logf, page_table,
           ffn1_norm, ffn1_w1, ffn1_w3, ffn1_w2, mix_norm, ffn2_norm, ffn2_w1, ffn2_w3, ffn2_w2,
           ple_norm, ple_w_gate, ple_w_proj, gmlp_w_in, gmlp_v_norm, gmlp_w_s, gmlp_b_s, gmlp_w_out,
           kv_norm, w_kvf, b_f, k_norm, att_w_q, q_norm, att_w_o):
    bf = lambda w: w.astype(BF)
    row = lambda g: g.reshape(1, -1)

    h = (x_prompt.reshape(NP, D), x_sample.reshape(NS, D))
    pe_p, pe_s = p_prompt.reshape(DEPTH, NP, PLE), p_sample.reshape(DEPTH, NS, PLE)

    tril = jnp.tril(jnp.ones((DH, DH), F32))
    w_prompt = gmlp_w_s * tril
    small = (gmlp_w_s[:, :, :DEC_T, :DEC_T] * tril[:DEC_T, :DEC_T])
    eye = jnp.eye(DH // DEC_T, dtype=F32)
    w_sample = jnp.einsum('ab,lgts->lgatbs', eye, small).reshape(N_A, NH, DH, DH)
    wmix = bf(jnp.stack([w_prompt, w_sample], axis=1))
    b_prompt = jnp.repeat(gmlp_b_s.transpose(0, 2, 1), DH, axis=2)
    b_sample = jnp.tile(b_prompt[:, :DEC_T], (1, DH // DEC_T, 1))
    bmix = jnp.stack([b_prompt, b_sample], axis=1)

    wk = bf(w_kvf[:, :D])
    wv = bf(w_kvf[:, D:2 * D])
    wf = bf(jnp.pad(w_kvf[:, 2 * D:].T, ((0, 16 - NH), (0, 0))))

    kv = None
    gmlp_vs = []
    for i in range(DEPTH):
        ffn1 = (row(ffn1_norm[i]), bf(ffn1_w1[i]), bf(ffn1_w3[i]), bf(ffn1_w2[i]))
        ffn2 = (row(ffn2_norm[i]), bf(ffn2_w1[i]), bf(ffn2_w3[i]), bf(ffn2_w2[i]))
        ple_args = (row(ple_norm[i]), bf(ple_w_gate[i]), pe_p, pe_s, i, bf(ple_w_proj[i]))
        if i < N_A:
            (h,) = _stage(h, ffn1)
            gmlp_args = (row(mix_norm[i]), bf(gmlp_w_in[i]), row(gmlp_v_norm[i]), wmix[i], bmix[i], bf(gmlp_w_out[i]))
            h, v_rows = _stage(h, ffn2, pre="gmlp", pre_args=gmlp_args, ple_args=ple_args)
            gmlp_vs.append(v_rows.reshape(DEC_B, DEC_T, D))
        else:
            j = i - N_A
            k32s, v32s, kb, vt, a, cst, forget = kv
            h, q = _stage(h, ffn1, q_args=(row(mix_norm[i]), bf(att_w_q[j]), row(jnp.tile(q_norm[j], NH))))
            o_p = _attn_prompt(q, kb, a, vt, cst.reshape(NH, 1, N))
            heads = lambda x: bf(x).reshape(DEC_B, ROWS, DH)
            pad_rows = lambda x: jnp.pad(x, ((0, 0), (0, 128 - ROWS), (0, 0)))
            c_new = cst[:, NP:].reshape(NH, DEC_B, DEC_T).transpose(1, 2, 0).reshape(DEC_B, ROWS)
            cq = jnp.broadcast_to(c_new[:, :, None], (DEC_B, ROWS, 128))
            ckn = jnp.pad(c_new, ((0, 0), (0, 128 - ROWS)))[:, None, :]
            o_s = _attn_sample(page_table, heads(q[NP:]), pad_rows(heads(k32s)), pad_rows(heads(v32s)),
                               cq, ckn, cache_k, cache_v, forget)
            (h,) = _stage(h, ffn2, pre="oproj", pre_args=(o_p, o_s.reshape(NS, D), bf(att_w_o[j])),
                          ple_args=ple_args, split_out=i == DEPTH - 1)
        if i == N_A - 1:
            k32p, k32s, v32p, v32s, kb, vt, a, lft, cst = _kv(
                h, row(kv_norm), wk, wv, wf,
                b_f.reshape(NH, 1), row(jnp.tile(k_norm, NH)))
            n_pool = cache_logf.shape[0]
            forget = _suffix(cache_logf.reshape(n_pool, PAGE * NH)).reshape(n_pool, 1, 2 * PAGE * NH)
            kv = (k32s, v32s, kb, vt, a, cst, forget)

    hp, hs = h
    lf_tok = lft.T
    return (
        hp.reshape(BATCH, SEQ, D),
        hs.reshape(DEC_B, DEC_T, D),
        k32p.reshape(BATCH, SEQ, NH, DH),
        v32p.reshape(BATCH, SEQ, NH, DH),
        lf_tok[:NP].reshape(BATCH, SEQ, NH),
        k32s.reshape(DEC_B, DEC_T, NH, DH),
        v32s.reshape(DEC_B, DEC_T, NH, DH),
        lf_tok[NP:].reshape(DEC_B, DEC_T, NH),
        jnp.stack(gmlp_vs),
    )
```

```python
import functools

import jax
import jax.numpy as jnp
from jax import lax
from jax.experimental import pallas as pl
from jax.experimental.pallas import tpu as pltpu

D = 1024
DFF = 2816
NH = 8
DH = 128
PLE = 256
DEPTH = 4
N_A = 2
BATCH, SEQ = 4, 4096
DEC_B, DEC_T = 128, 4
PAGE = 128
N_PAGES = 16
NP = BATCH * SEQ
NS = DEC_B * DEC_T
N = NP + NS
EPS = 1e-6
LOG2E = 1.4426950408889634
QSCALE = DH ** -0.5 * LOG2E

BF = jnp.bfloat16
F32 = jnp.float32

TM = 512
NPT = NP // TM
N_TILES = NPT + 1
FFN_CHUNKS = (256, 768, 768, 768, 256)
TQ = 512
SCORE_LOOKAHEAD = 2
PAGES_PER_STEP = 8
PAGE_SLOTS = 3
ROWS = DEC_T * NH
VMEM_LIMIT = 56 * 1024 * 1024

NT_DIMS = (((1,), (1,)), ((), ()))


def _params(*sem):
    return pltpu.CompilerParams(dimension_semantics=sem, vmem_limit_bytes=VMEM_LIMIT)


def _resident(shape):
    return pl.BlockSpec(shape, lambda *_: (0,) * len(shape), pipeline_mode=pl.Buffered(1))


def _tok_specs(width):
    return [pl.BlockSpec((TM, width), lambda i: (jnp.minimum(i, NPT - 1), 0)),
            pl.BlockSpec((NS, width), lambda i: (0, 0))]


def _tok_shapes(width, dtype):
    return [jax.ShapeDtypeStruct((NP, width), dtype), jax.ShapeDtypeStruct((NS, width), dtype)]


def _load_tok(p_ref, s_ref):
    return jnp.where(pl.program_id(0) == NPT, s_ref[...], p_ref[...])


def _store_tok(p_ref, s_ref, y):
    i = pl.program_id(0)

    @pl.when(i < NPT)
    def _():
        p_ref[...] = y

    @pl.when(i == NPT)
    def _():
        s_ref[...] = y


def _rms(x, g):
    ms = jnp.mean(x * x, axis=-1, keepdims=True)
    return x * lax.rsqrt(ms + EPS) * g


def _head_rms(x, g):
    outs = []
    for h in range(NH):
        xh = x[:, h * DH:(h + 1) * DH]
        ms = jnp.mean(xh * xh, axis=-1, keepdims=True)
        outs.append(xh * lax.rsqrt(ms + EPS))
    return jnp.concatenate(outs, axis=-1) * g


def _dot(a, b):
    return jnp.dot(a, b, preferred_element_type=F32)


def _dot_nt(a, b):
    return lax.dot_general(a, b, NT_DIMS, preferred_element_type=F32)


def _split3(x):
    hi = x.astype(BF).astype(F32)
    r = x - hi
    mid = r.astype(BF).astype(F32)
    return hi, mid, r - mid


def _gmlp_tile(x, g_ref, win_ref, vn_ref, wmix_ref, bmix_ref, wout_ref, vs_ref):
    xn = _rms(x, g_ref[...]).astype(BF)
    uv = _dot(xn, win_ref[...])
    u = uv[:, :D]
    v = _rms(uv[:, D:], vn_ref[...])

    @pl.when(pl.program_id(0) == NPT)
    def _():
        vs_ref[...] = v

    vb = v.astype(BF)
    bias = bmix_ref[...]
    rows = []
    for c in range(TM // DH):
        cols = []
        for g in range(NH):
            blk = vb[c * DH:(c + 1) * DH, g * DH:(g + 1) * DH]
            cols.append(_dot(wmix_ref[g], blk))
        rows.append(jnp.concatenate(cols, axis=1) + bias)
    s = jnp.concatenate(rows, axis=0)
    return x + _dot((u * s).astype(BF), wout_ref[...])


def _stage_kernel(pre, ple, want_q, split_in, split_out, *refs):
    refs = list(refs)
    take = lambda n: [refs.pop(0) for _ in range(n)]
    x_refs = take(2 if split_in else 1)
    pre_refs = take({None: 0, "gmlp": 6, "oproj": 3}[pre])
    g_ref, w1_ref, w3_ref, w2_ref = take(4)
    ple_refs = take(5 if ple else 0)
    q_refs = take(3 if want_q else 0)
    o_refs = take(2 if split_out else 1)
    vs_ref = take(1)[0] if pre == "gmlp" else None
    q_ref = take(1)[0] if want_q else None
    xn_ref, acc_ref = take(2)
    h_ref = take(1)[0] if pre else None

    x = _load_tok(*x_refs) if split_in else x_refs[0][...]
    if pre == "gmlp":
        x = _gmlp_tile(x, *pre_refs, vs_ref)
    elif pre == "oproj":
        op_ref, os_ref, wo_ref = pre_refs
        o_prompt = jnp.concatenate([op_ref[h] for h in range(NH)], axis=1)
        o = jnp.where(pl.program_id(0) == NPT, os_ref[...].astype(BF), o_prompt)
        x = x + _dot(o, wo_ref[...])
    if pre:
        h_ref[...] = x

    def chunk(f0, width):
        xn = xn_ref[...]
        h1 = _dot(xn, w1_ref[:, f0:f0 + width])
        h3 = _dot(xn, w3_ref[:, f0:f0 + width])
        a = (h1 * jax.nn.sigmoid(h1) * h3).astype(BF)
        return _dot(a, w2_ref[f0:f0 + width, :])

    xn_ref[...] = _rms(x, g_ref[...]).astype(BF)
    starts = [sum(FFN_CHUNKS[:c]) for c in range(len(FFN_CHUNKS))]
    acc_ref[...] = chunk(starts[0], FFN_CHUNKS[0])
    for f0, width in zip(starts[1:-1], FFN_CHUNKS[1:-1]):
        acc_ref[...] += chunk(f0, width)
    resid = h_ref[...] if pre else (_load_tok(*x_refs) if split_in else x_refs[0][...])
    y = resid + 0.5 * (acc_ref[...] + chunk(starts[-1], FFN_CHUNKS[-1]))

    if ple:
        gp_ref, wg_ref, pep_ref, pes_ref, wp_ref = ple_refs
        gate = jax.nn.sigmoid(_dot(_rms(y, gp_ref[...]).astype(BF), wg_ref[...]))
        y = y + gate * _dot(_load_tok(pep_ref, pes_ref).astype(BF), wp_ref[...])
    if split_out:
        _store_tok(*o_refs, y)
    else:
        o_refs[0][...] = y
    if want_q:
        gq_ref, wq_ref, qn_ref = q_refs
        q = _head_rms(_dot(_rms(y, gq_ref[...]).astype(BF), wq_ref[...]), qn_ref[...]) * QSCALE
        for h in range(NH):
            q_ref[h] = q[:, h * DH:(h + 1) * DH]


def _stage(x, ffn, pre=None, pre_args=(), ple_args=None, q_args=None, split_out=False):
    split_in = isinstance(x, tuple)
    tok = pl.BlockSpec((TM, D), lambda i: (i, 0))
    const = lambda i: (0, 0)
    args = list(x) if split_in else [x]
    in_specs = _tok_specs(D) if split_in else [tok]
    if pre == "gmlp":
        g, w_in, v_norm, wmix, bmix, w_out = pre_args
        args += [g, w_in, v_norm, wmix, bmix, w_out]
        in_specs += [_resident((1, D)), _resident((D, 2 * D)), _resident((1, D)),
                     pl.BlockSpec((None, NH, DH, DH), lambda i: (i // NPT, 0, 0, 0)),
                     pl.BlockSpec((None, DH, D), lambda i: (i // NPT, 0, 0)),
                     _resident((D, D))]
    elif pre == "oproj":
        args += list(pre_args)
        in_specs += [pl.BlockSpec((NH, TM, DH), lambda i: (0, jnp.minimum(i, NPT - 1), 0)),
                     pl.BlockSpec((NS, D), const), _resident((D, D))]
    args += list(ffn)
    in_specs += [_resident((1, D)), _resident((D, DFF)), _resident((D, DFF)), _resident((DFF, D))]
    if ple_args is not None:
        g, w_gate, pe_p, pe_s, layer, w_proj = ple_args
        args += [g, w_gate, pe_p, pe_s, w_proj]
        in_specs += [_resident((1, D)), _resident((D, D)),
                     pl.BlockSpec((None, TM, PLE), lambda i: (layer, jnp.minimum(i, NPT - 1), 0)),
                     pl.BlockSpec((None, NS, PLE), lambda i: (layer, 0, 0)),
                     _resident((PLE, D))]
    if q_args is not None:
        args += list(q_args)
        in_specs += [_resident((1, D)), _resident((D, D)), _resident((1, D))]
    out_specs = _tok_specs(D) if split_out else [tok]
    out_shape = _tok_shapes(D, F32) if split_out else [jax.ShapeDtypeStruct((N, D), F32)]
    if pre == "gmlp":
        out_specs.append(pl.BlockSpec((NS, D), const))
        out_shape.append(jax.ShapeDtypeStruct((NS, D), F32))
    if q_args is not None:
        out_specs.append(pl.BlockSpec((NH, TM, DH), lambda i: (0, i, 0)))
        out_shape.append(jax.ShapeDtypeStruct((NH, N, DH), F32))
    scratch = [pltpu.VMEM((TM, D), BF), pltpu.VMEM((TM, D), F32)]
    if pre:
        scratch.append(pltpu.VMEM((TM, D), F32))
    out = pl.pallas_call(
        functools.partial(_stage_kernel, pre, ple_args is not None, q_args is not None, split_in, split_out),
        grid=(N_TILES,),
        in_specs=in_specs,
        out_specs=out_specs,
        out_shape=out_shape,
        scratch_shapes=scratch,
        compiler_params=_params("arbitrary"),
        name="stage",
    )(*args)
    if split_out:
        return ((out[0], out[1]),) + tuple(out[2:])
    return tuple(out)


def _kv_kernel(x_ref, g_ref, wk_ref, wv_ref, wf_ref, bf_ref, kn_ref,
               k32p_ref, k32s_ref, v32p_ref, v32s_ref, kb_ref, vt_ref, a_ref, lf_ref, cs_ref, carry_ref):
    i = pl.program_id(0)
    is_sample = i == NPT
    xs = _rms(x_ref[...], g_ref[...]).astype(BF)
    k = _head_rms(_dot(xs, wk_ref[...]), kn_ref[...])
    v = _dot(xs, wv_ref[...])
    for h in range(NH):
        kb_ref[h] = k[:, h * DH:(h + 1) * DH].astype(BF)
    vt_ref[...] = jnp.transpose(v).astype(BF)

    f = _dot_nt(wf_ref[...], xs)[:NH] + bf_ref[...]
    lf = jnp.minimum(f, 0.0) - jnp.log1p(jnp.exp(-jnp.abs(f)))
    lf_ref[...] = lf

    lane = lax.broadcasted_iota(jnp.int32, (NH, TM), 1)
    pos = lane & jnp.where(is_sample, DEC_T - 1, TM - 1)
    y = lf
    sh = 1
    while sh < TM:
        y = y + jnp.where(pos >= sh, pltpu.roll(y, sh, 1), 0.0)
        sh *= 2

    @pl.when(jnp.logical_or(i % (SEQ // TM) == 0, is_sample))
    def _():
        carry_ref[...] = jnp.zeros_like(carry_ref)

    y = y + carry_ref[:, 0:1]
    cs_ref[...] = y
    carry_ref[...] = jnp.broadcast_to(y[:, TM - 1:TM], carry_ref.shape)

    hi, mid, lo = _split3(-(y * LOG2E))
    row = lax.broadcasted_iota(jnp.int32, (NH, TM), 0)
    ones = jnp.where(row < 3, 1.0, 0.0)
    pad = jnp.zeros((DH - 4 * NH, TM), F32)
    a_ref[...] = jnp.transpose(jnp.concatenate([hi, mid, lo, ones, pad], axis=0)).astype(BF)

    _store_tok(k32p_ref, k32s_ref, k)
    _store_tok(v32p_ref, v32s_ref, v)


def _kv(h, g, wk, wv, wf, bf, kn):
    const = lambda i: (0, 0)
    tile = lambda i: (i, 0)
    return pl.pallas_call(
        _kv_kernel,
        grid=(N_TILES,),
        in_specs=[
            pl.BlockSpec((TM, D), lambda i: (i, 0)),
            pl.BlockSpec((1, D), const),
            pl.BlockSpec((D, D), const),
            pl.BlockSpec((D, D), const),
            pl.BlockSpec((16, D), const),
            pl.BlockSpec((NH, 1), const),
            pl.BlockSpec((1, D), const),
        ],
        out_specs=_tok_specs(D) + _tok_specs(D) + [
            pl.BlockSpec((NH, TM, DH), lambda i: (0, i, 0)),
            pl.BlockSpec((D, TM), lambda i: (0, i)),
            pl.BlockSpec((TM, DH), tile),
            pl.BlockSpec((NH, TM), lambda i: (0, i)),
            pl.BlockSpec((NH, TM), lambda i: (0, i)),
        ],
        out_shape=_tok_shapes(D, F32) + _tok_shapes(D, F32) + [
            jax.ShapeDtypeStruct((NH, N, DH), BF),
            jax.ShapeDtypeStruct((D, N), BF),
            jax.ShapeDtypeStruct((N, DH), BF),
            jax.ShapeDtypeStruct((NH, N), F32),
            jax.ShapeDtypeStruct((NH, N), F32),
        ],
        scratch_shapes=[pltpu.VMEM((NH, 128), F32)],
        compiler_params=_params("arbitrary"),
        name="shared_kv",
    )(h, g, wk, wv, wf, bf, kn)


def _attn_prompt_kernel(q_ref, k_ref, a_ref, vt_ref, c_ref, o_ref):
    h = pl.program_id(1)
    row = lax.broadcasted_iota(jnp.int32, (DH, TQ), 0)
    own = jnp.logical_and(row < 3 * NH, (row & (NH - 1)) == h)
    key = lax.broadcasted_iota(jnp.int32, (TQ, TQ), 0)
    qry = lax.broadcasted_iota(jnp.int32, (TQ, TQ), 1)

    def query_block(qi):
        q0 = qi * TQ
        hi, mid, lo = _split3(c_ref[0, :, q0:q0 + TQ] * LOG2E)
        aug = jnp.where(row == 3 * NH, hi, jnp.where(row == 3 * NH + 1, mid, jnp.where(row == 3 * NH + 2, lo, 0.0)))
        aug = jnp.where(own, 1.0, aug)
        return jnp.concatenate([jnp.transpose(q_ref[q0:q0 + TQ, :]), aug], axis=0).astype(BF)

    def scores(qa, qi, g):
        k0 = g * TQ
        kaug = jnp.concatenate([k_ref[k0:k0 + TQ, :], a_ref[k0:k0 + TQ, :]], axis=1)
        s = _dot(kaug, qa)
        return jnp.where(key <= qry, s, -jnp.inf) if g == qi else s

    pairs = [(qi, g) for qi in range(SEQ // TQ) for g in range(qi + 1)]
    qa = None
    queue = []

    def issue_scores(n):
        nonlocal qa
        if n < len(pairs):
            qi1, g1 = pairs[n]
            if g1 == 0:
                qa = query_block(qi1)
            queue.append(scores(qa, qi1, g1))

    for n in range(SCORE_LOOKAHEAD):
        issue_scores(n)
    m = l = acc = pending = None
    for n, (qi, g) in enumerate(pairs):
        issue_scores(n + SCORE_LOOKAHEAD)
        s = queue.pop(0)
        m_cur = jnp.max(s, axis=0, keepdims=True)
        if g == 0:
            m, alpha = m_cur, None
            p = jnp.exp2(s - m)
            l = jnp.sum(p, axis=0, keepdims=True)
        else:
            m_new = jnp.maximum(m, m_cur)
            alpha = jnp.exp2(m - m_new)
            p = jnp.exp2(s - m_new)
            l = alpha * l + jnp.sum(p, axis=0, keepdims=True)
            m = m_new
        if pending is not None:
            acc = pending()
        k0 = g * TQ

        def pending(acc=acc, alpha=alpha, p=p, k0=k0, qi=qi, g=g, l=l):
            pv = _dot(vt_ref[:, k0:k0 + TQ], p.astype(BF))
            new = pv if alpha is None else alpha * acc + pv
            if g == qi:
                o_ref[qi * TQ:(qi + 1) * TQ, :] = jnp.transpose(new / l).astype(BF)
            return new
    pending()


def _attn_prompt(q, kb, a, vt, cs3):
    return pl.pallas_call(
        _attn_prompt_kernel,
        grid=(BATCH, NH),
        in_specs=[
            pl.BlockSpec((None, SEQ, DH), lambda b, h: (h, b, 0)),
            pl.BlockSpec((None, SEQ, DH), lambda b, h: (h, b, 0)),
            pl.BlockSpec((SEQ, DH), lambda b, h: (b, 0)),
            pl.BlockSpec((DH, SEQ), lambda b, h: (h, b)),
            pl.BlockSpec((1, 1, SEQ), lambda b, h: (h, 0, b)),
        ],
        out_specs=pl.BlockSpec((None, SEQ, DH), lambda b, h: (h, b, 0)),
        out_shape=jax.ShapeDtypeStruct((NH, NP, DH), BF),
        compiler_params=_params("parallel", "parallel"),
        name="attn_prompt",
    )(q, kb, a, vt, cs3)


def _suffix_kernel(x_ref, o_ref):
    x = x_ref[...]
    width = PAGE * NH
    lane = lax.broadcasted_iota(jnp.int32, x.shape, 1)
    y = x
    t = x
    sh = NH
    while sh < width:
        y = y + jnp.where(lane < width - sh, pltpu.roll(y, width - sh, 1), 0.0)
        t = t + pltpu.roll(t, sh, 1)
        sh *= 2
    o_ref[:, :width] = y - x
    o_ref[:, width:] = t


def _suffix(lf_rows):
    rows, width = lf_rows.shape
    tr = 256
    spec = pl.BlockSpec((tr, width), lambda i: (i, 0))
    return pl.pallas_call(
        _suffix_kernel,
        grid=(rows // tr,),
        in_specs=[spec],
        out_specs=pl.BlockSpec((tr, 2 * width), lambda i: (i, 0)),
        out_shape=jax.ShapeDtypeStruct((rows, 2 * width), F32),
        compiler_params=_params("parallel"),
        name="cache_suffix",
    )(lf_rows)


def _attn_sample_kernel(pt_ref, q_ref, kn_ref, vn_ref, cq_ref, ckn_ref, ck_hbm, cv_hbm, f_hbm, o_ref,
                        kbuf, vbuf, fbuf, sems, m_ref, l_ref, acc_ref, carry_ref):
    pp = PAGES_PER_STEP
    steps_per_seq = N_PAGES // pp
    n_steps = DEC_B * steps_per_seq
    b = pl.program_id(0)
    p = pl.program_id(1)
    t = b * steps_per_seq + p

    def page_copies(step, slot):
        sb = step // steps_per_seq
        sp = step % steps_per_seq
        copies = []
        for j in range(pp):
            page = pt_ref[sb, N_PAGES - 1 - (pp * sp + j)]
            copies.append(pltpu.make_async_copy(ck_hbm.at[page], kbuf.at[slot, j], sems.at[0, slot]))
            copies.append(pltpu.make_async_copy(cv_hbm.at[page], vbuf.at[slot, j], sems.at[1, slot]))
            copies.append(pltpu.make_async_copy(f_hbm.at[page], fbuf.at[slot, j], sems.at[2, slot]))
        return copies

    def fetch(step, slot):
        for c in page_copies(step, slot):
            c.start()

    @pl.when(t == 0)
    def _():
        fetch(0, 0)
        fetch(1, 1)

    @pl.when(t + 2 < n_steps)
    def _():
        fetch(t + 2, (t + 2) % PAGE_SLOTS)

    slot = t % PAGE_SLOTS
    for c in page_copies(t, slot):
        c.wait()

    q = q_ref[0]
    cq = cq_ref[0][:, 0:1] * LOG2E

    @pl.when(p == 0)
    def _():
        s = _dot_nt(q, kn_ref[0])
        s = s + cq - ckn_ref[0] * LOG2E
        row = lax.broadcasted_iota(jnp.int32, s.shape, 0)
        col = lax.broadcasted_iota(jnp.int32, s.shape, 1)
        ok = jnp.logical_and((col & (NH - 1)) == (row & (NH - 1)), (col >> 3) <= (row >> 3))
        s = jnp.where(ok, s, -jnp.inf)
        m = jnp.max(s, axis=1, keepdims=True)
        e = jnp.exp2(s - m)
        m_ref[...] = m
        l_ref[...] = jnp.sum(e, axis=1, keepdims=True)
        acc_ref[...] = _dot(e.astype(BF), vn_ref[0])
        carry_ref[...] = jnp.zeros_like(carry_ref)

    carry = carry_ref[...]
    tails = []
    for j in range(pp):
        tails.append(fbuf[slot, j, :, :PAGE * NH] + carry)
        carry = carry + fbuf[slot, j, :, PAGE * NH:]
    carry_ref[...] = carry
    tail = jnp.concatenate(tails, axis=1) * LOG2E
    k = jnp.concatenate([kbuf[slot, j].reshape(PAGE * NH, DH).astype(BF) for j in range(pp)], axis=0)
    v = jnp.concatenate([vbuf[slot, j].reshape(PAGE * NH, DH).astype(BF) for j in range(pp)], axis=0)
    s = _dot_nt(q, k)
    row = lax.broadcasted_iota(jnp.int32, s.shape, 0)
    col = lax.broadcasted_iota(jnp.int32, s.shape, 1)
    s = jnp.where((col & (NH - 1)) == (row & (NH - 1)), s + cq + tail, -jnp.inf)
    m_prev = m_ref[...]
    m_new = jnp.maximum(m_prev, jnp.max(s, axis=1, keepdims=True))
    alpha = jnp.exp2(m_prev - m_new)
    e = jnp.exp2(s - m_new)
    l_ref[...] = alpha * l_ref[...] + jnp.sum(e, axis=1, keepdims=True)
    acc_ref[...] = alpha * acc_ref[...] + _dot(e.astype(BF), v)
    m_ref[...] = m_new

    @pl.when(p == steps_per_seq - 1)
    def _():
        o_ref[0] = acc_ref[...] / l_ref[...]


def _attn_sample(page_table, q, kn, vn, cq, ckn, cache_k, cache_v, forget):
    pp = PAGES_PER_STEP

    def seq(b, p, pt):
        return (b, 0, 0)

    hbm = pl.BlockSpec(memory_space=pl.ANY)
    grid_spec = pltpu.PrefetchScalarGridSpec(
        num_scalar_prefetch=1,
        grid=(DEC_B, N_PAGES // pp),
        in_specs=[
            pl.BlockSpec((1, ROWS, DH), seq),
            pl.BlockSpec((1, 128, DH), seq),
            pl.BlockSpec((1, 128, DH), seq),
            pl.BlockSpec((1, ROWS, 128), seq),
            pl.BlockSpec((1, 1, 128), seq),
            hbm, hbm, hbm,
        ],
        out_specs=pl.BlockSpec((1, ROWS, DH), seq),
        scratch_shapes=[
            pltpu.VMEM((PAGE_SLOTS, pp, PAGE, NH, DH), F32),
            pltpu.VMEM((PAGE_SLOTS, pp, PAGE, NH, DH), F32),
            pltpu.VMEM((PAGE_SLOTS, pp, 1, 2 * PAGE * NH), F32),
            pltpu.SemaphoreType.DMA((3, PAGE_SLOTS)),
            pltpu.VMEM((ROWS, 1), F32),
            pltpu.VMEM((ROWS, 1), F32),
            pltpu.VMEM((ROWS, DH), F32),
            pltpu.VMEM((1, PAGE * NH), F32),
        ],
    )
    return pl.pallas_call(
        _attn_sample_kernel,
        grid_spec=grid_spec,
        out_shape=jax.ShapeDtypeStruct((DEC_B, ROWS, DH), F32),
        compiler_params=_params("arbitrary", "arbitrary"),
        name="attn_sample",
    )(page_table, q, kn, vn, cq, ckn, cache_k, cache_v, forget)


def kernel(x_prompt, x_sample, p_prompt, p_sample, cache_k, cache_v, cache_logf, page_table,
           ffn1_norm, ffn1_w1, ffn1_w3, ffn1_w2, mix_norm, ffn2_norm, ffn2_w1, ffn2_w3, ffn2_w2,
           ple_norm, ple_w_gate, ple_w_proj, gmlp_w_in, gmlp_v_norm, gmlp_w_s, gmlp_b_s, gmlp_w_out,
           kv_norm, w_kvf, b_f, k_norm, att_w_q, q_norm, att_w_o):
    bf = lambda w: w.astype(BF)
    row = lambda g: g.reshape(1, -1)

    h = (x_prompt.reshape(NP, D), x_sample.reshape(NS, D))
    pe_p, pe_s = p_prompt.reshape(DEPTH, NP, PLE), p_sample.reshape(DEPTH, NS, PLE)

    tril = jnp.tril(jnp.ones((DH, DH), F32))
    w_prompt = gmlp_w_s * tril
    small = (gmlp_w_s[:, :, :DEC_T, :DEC_T] * tril[:DEC_T, :DEC_T])
    eye = jnp.eye(DH // DEC_T, dtype=F32)
    w_sample = jnp.einsum('ab,lgts->lgatbs', eye, small).reshape(N_A, NH, DH, DH)
    wmix = bf(jnp.stack([w_prompt, w_sample], axis=1))
    b_prompt = jnp.repeat(gmlp_b_s.transpose(0, 2, 1), DH, axis=2)
    b_sample = jnp.tile(b_prompt[:, :DEC_T], (1, DH // DEC_T, 1))
    bmix = jnp.stack([b_prompt, b_sample], axis=1)

    wk = bf(w_kvf[:, :D])
    wv = bf(w_kvf[:, D:2 * D])
    wf = bf(jnp.pad(w_kvf[:, 2 * D:].T, ((0, 16 - NH), (0, 0))))

    kv = None
    gmlp_vs = []
    for i in range(DEPTH):
        ffn1 = (row(ffn1_norm[i]), bf(ffn1_w1[i]), bf(ffn1_w3[i]), bf(ffn1_w2[i]))
        ffn2 = (row(ffn2_norm[i]), bf(ffn2_w1[i]), bf(ffn2_w3[i]), bf(ffn2_w2[i]))
        ple_args = (row(ple_norm[i]), bf(ple_w_gate[i]), pe_p, pe_s, i, bf(ple_w_proj[i]))
        if i < N_A:
            (h,) = _stage(h, ffn1)
            gmlp_args = (row(mix_norm[i]), bf(gmlp_w_in[i]), row(gmlp_v_norm[i]), wmix[i], bmix[i], bf(gmlp_w_out[i]))
            h, v_rows = _stage(h, ffn2, pre="gmlp", pre_args=gmlp_args, ple_args=ple_args)
            gmlp_vs.append(v_rows.reshape(DEC_B, DEC_T, D))
        else:
            j = i - N_A
            k32s, v32s, kb, vt, a, cst, forget = kv
            h, q = _stage(h, ffn1, q_args=(row(mix_norm[i]), bf(att_w_q[j]), row(jnp.tile(q_norm[j], NH))))
            o_p = _attn_prompt(q, kb, a, vt, cst.reshape(NH, 1, N))
            heads = lambda x: bf(x).reshape(DEC_B, ROWS, DH)
            pad_rows = lambda x: jnp.pad(x, ((0, 0), (0, 128 - ROWS), (0, 0)))
            c_new = cst[:, NP:].reshape(NH, DEC_B, DEC_T).transpose(1, 2, 0).reshape(DEC_B, ROWS)
            cq = jnp.broadcast_to(c_new[:, :, None], (DEC_B, ROWS, 128))
            ckn = jnp.pad(c_new, ((0, 0), (0, 128 - ROWS)))[:, None, :]
            q_s = q[:, NP:].transpose(1, 0, 2)
            o_s = _attn_sample(page_table, heads(q_s), pad_rows(heads(k32s)), pad_rows(heads(v32s)),
                               cq, ckn, cache_k, cache_v, forget)
            (h,) = _stage(h, ffn2, pre="oproj", pre_args=(o_p, o_s.reshape(NS, D), bf(att_w_o[j])),
                          ple_args=ple_args, split_out=i == DEPTH - 1)
        if i == N_A - 1:
            k32p, k32s, v32p, v32s, kb, vt, a, lft, cst = _kv(
                h, row(kv_norm), wk, wv, wf,
                b_f.reshape(NH, 1), row(jnp.tile(k_norm, NH)))
            n_pool = cache_logf.shape[0]
            forget = _suffix(cache_logf.reshape(n_pool, PAGE * NH)).reshape(n_pool, 1, 2 * PAGE * NH)
            kv = (k32s, v32s, kb, vt, a, cst, forget)

    hp, hs = h
    lf_tok = lft.T
    return (
        hp.reshape(BATCH, SEQ, D),
        hs.reshape(DEC_B, DEC_T, D),
        k32p.reshape(BATCH, SEQ, NH, DH),
        v32p.reshape(BATCH, SEQ, NH, DH),
        lf_tok[:NP].reshape(BATCH, SEQ, NH),
        k32s.reshape(DEC_B, DEC_T, NH, DH),
        v32s.reshape(DEC_B, DEC_T, NH, DH),
        lf_tok[NP:].reshape(DEC_B, DEC_T, NH),
        jnp.stack(gmlp_vs),
    )
```
